```python
import jax, jax.numpy as jnp
from jax import lax
import numpy as np

D_MODEL = 2048
BATCH = 4
SEQ = 4096
DEPTH = 1
DEC_BATCH = 4
DEC_SEQ = 8192
PAST_LEN = 128

N_META = 16
GRID_W = 64
NH_A = 8
DK_A = 128
DV_A = 256
QK_A = NH_A * DK_A
D_A = NH_A * DV_A
MLSTM_CHUNK = 64
CONV_W = 3
NH_B = 16
DH_B = 64
D_B = NH_B * DH_B
KR_MAX = 8
KC = 16
Q_COL_BLOCK = 16
K_COL_SPAN = 32
N_GROUPS = 4
EXPERTS_PER_GROUP = 8
N_EXPERTS = N_GROUPS * EXPERTS_PER_GROUP
TOP_K = 2
D_EXPERT = 1024
MOE_BLOCK = 128
ALPHA = (2 * DEPTH) ** 0.25
BETA = (8 * DEPTH) ** -0.25
LN_EPS = 1e-5
NEG_INF = -1e30
IN_SPLITS = (QK_A, QK_A, D_A, D_A, 4 * NH_A, D_B, D_B, D_B, D_MODEL, D_MODEL)
P_IN = sum(IN_SPLITS)

kernel_name = 'hybrid_mlstm_natten_hmoe_encoder'


def layer_norm(x, g, b):
    xf = x.astype(jnp.float32)
    xc = xf - xf.mean(-1, keepdims=True)
    var = jnp.mean(xc * xc, -1, keepdims=True)
    return (xc * lax.rsqrt(var + LN_EPS) * g.astype(jnp.float32) + b.astype(jnp.float32)).astype(x.dtype)


def to_heads(a, nh):
    B, L, _ = a.shape
    return a.reshape(B, L, nh, -1).transpose(0, 2, 1, 3)


def from_heads(h):
    B, H, L, d = h.shape
    return h.transpose(0, 2, 1, 3).reshape(B, L, H * d)


def centred_conv(x, w, b):
    L = x.shape[1]
    pad = CONV_W // 2
    xp = jnp.pad(x, ((0, 0), (pad, pad), (0, 0)))
    return sum(xp[:, i:i + L] * w[i] for i in range(CONV_W)) + b


def mlstm_chunks(q, k, v, ig, lf, chunk, state):
    B, H, L, _ = q.shape
    dv = v.shape[-1]
    nc = L // chunk

    def to_chunks(a):
        return jnp.moveaxis(a.reshape((B, H, nc, chunk) + a.shape[3:]), 2, 0)

    causal = jnp.tril(jnp.ones((chunk, chunk), dtype=bool))

    def body(carry, xs):
        C, n, m = carry
        qc, kc, vc, ic, fc = xs
        b = jnp.cumsum(fc, axis=-1)
        log_d = jnp.where(causal, b[..., :, None] - b[..., None, :] + ic[..., None, :], NEG_INF)
        log_inter = b + m[..., None]
        m_t = jnp.maximum(log_inter, log_d.max(-1))
        s = jnp.einsum('bhtd,bhsd->bhts', qc, kc) * jnp.exp(log_d - m_t[..., None])
        w_inter = jnp.exp(log_inter - m_t)
        num = jnp.einsum('bhts,bhsv->bhtv', s, vc) + w_inter[..., None] * jnp.einsum('bhtd,bhdv->bhtv', qc, C)
        den = s.sum(-1) + w_inter * jnp.einsum('bhtd,bhd->bht', qc, n)
        h = num / jnp.maximum(jnp.abs(den), jnp.exp(-m_t))[..., None]
        b_last = b[..., -1]
        log_w = b_last[..., None] - b + ic
        m_new = jnp.maximum(b_last + m, log_w.max(-1))
        w = jnp.exp(log_w - m_new[..., None])
        decay = jnp.exp(b_last + m - m_new)
        C = decay[..., None, None] * C + jnp.einsum('bhsd,bhsv->bhdv', kc * w[..., None], vc)
        n = decay[..., None] * n + jnp.einsum('bhs,bhsd->bhd', w, kc)
        return (C, n, m_new), h

    state, hs = lax.scan(body, state, tuple(to_chunks(a) for a in (q, k, v, ig, lf)))
    return jnp.moveaxis(hs, 0, 2).reshape(B, H, L, dv), state


def mlstm_bidirectional(q, k, v, ig_f, lf_f, ig_b, lf_b):
    B, H, _, dk = q.shape
    dv = v.shape[-1]
    state0 = (jnp.zeros((B, H, dk, dv), jnp.float32), jnp.zeros((B, H, dk), jnp.float32),
              jnp.zeros((B, H), jnp.float32))
    meta = lambda a: a[:, :, :N_META]
    real = lambda a: a[:, :, N_META:]
    flip = lambda a: jnp.flip(a, axis=2)
    h_meta_f, st = mlstm_chunks(*(meta(a) for a in (q, k, v, ig_f, lf_f)), N_META, state0)
    h_real_f, _ = mlstm_chunks(*(real(a) for a in (q, k, v, ig_f, lf_f)), MLSTM_CHUNK, st)
    h_real_b, st = mlstm_chunks(*(flip(real(a)) for a in (q, k, v, ig_b, lf_b)), MLSTM_CHUNK, state0)
    h_meta_b, _ = mlstm_chunks(*(flip(meta(a)) for a in (q, k, v, ig_b, lf_b)), N_META, st)
    h_f = jnp.concatenate([h_meta_f, h_real_f], axis=2)
    h_b = jnp.concatenate([flip(h_meta_b), flip(h_real_b)], axis=2)
    return h_f + h_b


def head_norm(h, g):
    hc = h - h.mean(-1, keepdims=True)
    var = jnp.mean(hc * hc, -1, keepdims=True)
    return hc * lax.rsqrt(var + LN_EPS) * g.astype(jnp.float32).reshape(NH_A, 1, DV_A)


def neighbourhood_attention(q, k, v, rpb):
    B, H, L, d = q.shape
    T = L - N_META
    rows = T // GRID_W
    kr = min(KR_MAX, rows)
    scale = DH_B ** -0.5
    f32 = jnp.float32
    qm, km, vm = q[:, :, :N_META], k[:, :, :N_META], v[:, :, :N_META]
    qg = q[:, :, N_META:].reshape(B, H, rows, GRID_W, d)
    kg = k[:, :, N_META:].reshape(B, H, rows, GRID_W, d)
    vg = v[:, :, N_META:].reshape(B, H, rows, GRID_W, d)
    pm = jax.nn.softmax((jnp.einsum('bhqd,bhkd->bhqk', qm, km) * scale).astype(f32), axis=-1)
    om = jnp.einsum('bhqk,bhkd->bhqd', pm.astype(v.dtype), vm)
    cols = np.arange(GRID_W)
    cstart = np.clip(cols - KC // 2, 0, GRID_W - KC)
    ncb = GRID_W // Q_COL_BLOCK
    ustart = np.clip(cstart[::Q_COL_BLOCK], 0, GRID_W - K_COL_SPAN)
    key_cols = ustart[:, None] + np.arange(K_COL_SPAN)[None, :]
    qcols = cols.reshape(ncb, Q_COL_BLOCK)
    qs = cstart.reshape(ncb, Q_COL_BLOCK)
    kc3 = key_cols[:, None, :]
    col_valid = (kc3 >= qs[:, :, None]) & (kc3 < qs[:, :, None] + KC)
    col_off = np.clip(kc3 - qcols[:, :, None] + KC - 1, 0, 2 * KC - 2)
    rpb_c = rpb[:, :, col_off]

    def row(r):
        rs = jnp.clip(r - kr // 2, 0, rows - kr)
        kw = lax.dynamic_slice_in_dim(kg, rs, kr, axis=2)[:, :, :, key_cols]
        vw = lax.dynamic_slice_in_dim(vg, rs, kr, axis=2)[:, :, :, key_cols]
        qr = lax.dynamic_index_in_dim(qg, r, axis=2, keepdims=False).reshape(B, H, ncb, Q_COL_BLOCK, d)
        bias = lax.dynamic_slice_in_dim(rpb_c, rs - r + KR_MAX - 1, kr, axis=1)
        s_win = (jnp.einsum('bhjqd,bhrjkd->bhjqrk', qr, kw) * scale).astype(f32) \
            + bias.transpose(0, 2, 3, 1, 4).astype(f32)[None]
        s_win = jnp.where(col_valid[:, :, None, :], s_win, NEG_INF).reshape(B, H, ncb, Q_COL_BLOCK, kr * K_COL_SPAN)
        s_meta = (jnp.einsum('bhjqd,bhmd->bhjqm', qr, km) * scale).astype(f32)
        p = jax.nn.softmax(jnp.concatenate([s_meta, s_win], axis=-1), axis=-1).astype(v.dtype)
        p_win = p[..., N_META:].reshape(B, H, ncb, Q_COL_BLOCK, kr, K_COL_SPAN)
        out = jnp.einsum('bhjqm,bhmd->bhjqd', p[..., :N_META], vm) \
            + jnp.einsum('bhjqrk,bhrjkd->bhjqd', p_win, vw)
        return out.reshape(B, H, GRID_W, d)

    og = lax.map(row, jnp.arange(rows, dtype=jnp.int32))
    og = jnp.moveaxis(og, 0, 2).reshape(B, H, T, d)
    return jnp.concatenate([om, og], axis=2)


def token_mixer(x, w_in, b_gates, conv_w, conv_b, head_g, rpb, w_a, w_b, w_out):
    B, L, _ = x.shape
    proj = x @ w_in
    qa, ka, va, oa, gates, qb, kb, vb, ga, gb = jnp.split(proj, np.cumsum(IN_SPLITS)[:-1].tolist(), axis=-1)
    qk = jax.nn.silu(centred_conv(jnp.concatenate([qa, ka], axis=-1), conv_w, conv_b))
    qa, ka = jnp.split(qk, 2, axis=-1)
    q = to_heads(qa, NH_A).astype(jnp.float32)
    k = to_heads(ka, NH_A).astype(jnp.float32) * DK_A ** -0.5
    v = to_heads(va, NH_A).astype(jnp.float32)
    g = (gates + b_gates).astype(jnp.float32).reshape(B, L, 4, NH_A).transpose(2, 0, 3, 1)
    h = mlstm_bidirectional(q, k, v, g[0], jax.nn.log_sigmoid(g[2]), g[1], jax.nn.log_sigmoid(g[3]))
    y_a = from_heads(head_norm(h, head_g)).astype(x.dtype) * jax.nn.sigmoid(oa)
    y_b = from_heads(neighbourhood_attention(to_heads(qb, NH_B), to_heads(kb, NH_B), to_heads(vb, NH_B), rpb))
    merged = jax.nn.sigmoid(ga) * (y_a @ w_a) + jax.nn.sigmoid(gb) * (y_b @ w_b)
    return merged @ w_out


def hier_moe(x, router_g_w, router_g_b, router_e_w, router_e_b, w_gate, w_up, w_down):
    B, L, D = x.shape
    xt = x.reshape(-1, D)
    N = xt.shape[0]
    glog = (xt @ router_g_w + router_g_b).astype(jnp.float32)
    gsel = jnp.argmax(glog, axis=-1).astype(jnp.int32)
    pg = jnp.take_along_axis(jax.nn.softmax(glog, axis=-1), gsel[:, None], axis=1)[:, 0]
    elog = (xt @ router_e_w + router_e_b).astype(jnp.float32).reshape(N, N_GROUPS, EXPERTS_PER_GROUP)
    elog_sel = jnp.take_along_axis(elog, gsel[:, None, None], axis=1)[:, 0]
    tv, ti = lax.top_k(elog_sel, TOP_K)
    gate = pg[:, None] * jax.nn.softmax(tv, axis=-1)
    eid = gsel[:, None] * EXPERTS_PER_GROUP + ti.astype(jnp.int32)
    A = N * TOP_K
    flat_e = eid.reshape(-1)
    flat_t = jnp.repeat(jnp.arange(N, dtype=jnp.int32), TOP_K)
    flat_g = gate.reshape(-1)
    order = jnp.argsort(flat_e)
    se = flat_e[order]
    counts = jnp.zeros((N_EXPERTS,), jnp.int32).at[flat_e].add(1)
    pcounts = (counts + MOE_BLOCK - 1) // MOE_BLOCK * MOE_BLOCK
    pend = jnp.cumsum(pcounts)
    pstart = pend - pcounts
    cstart = jnp.cumsum(counts) - counts
    dest = pstart[se] + jnp.arange(A, dtype=jnp.int32) - cstart[se]
    nb = -(-A // MOE_BLOCK) + N_EXPERTS
    P = nb * MOE_BLOCK
    buf_tok = jnp.full((P,), N, jnp.int32).at[dest].set(flat_t[order])
    buf_gate = jnp.zeros((P,), xt.dtype).at[dest].set(flat_g[order].astype(xt.dtype))
    block_e = jnp.minimum(jnp.searchsorted(pend, jnp.arange(nb, dtype=jnp.int32) * MOE_BLOCK, side='right'),
                          N_EXPERTS - 1)
    x_pad = jnp.concatenate([xt, jnp.zeros((1, D), xt.dtype)], axis=0)

    def run_block(args):
        tok, e = args
        xb = x_pad[tok]
        hb = jax.nn.silu(xb @ w_gate[e]) * (xb @ w_up[e])
        return hb @ w_down[e]

    yb = lax.map(run_block, (buf_tok.reshape(nb, MOE_BLOCK), block_e))
    y = jnp.zeros((N + 1, D), xt.dtype).at[buf_tok].add(yb.reshape(P, D) * buf_gate[:, None])
    return y[:N].reshape(B, L, D)


def setup_inputs(seed: int = 0) -> dict:
    key = jax.random.key(seed)
    ks = jax.random.split(key, 25)
    nrm = lambda k, shape, s: jax.random.normal(k, shape, jnp.float32) * s
    col_scale = np.concatenate([np.full((n,), BETA if i in (2, 7) else 1.0, np.float32)
                                for i, n in enumerate(IN_SPLITS)])
    fgate_bias = np.concatenate([np.zeros((2 * NH_A,), np.float32),
                                 np.tile(np.linspace(3.0, 6.0, NH_A, dtype=np.float32), 2)])
    return {
        'x_prompt': nrm(ks[0], (BATCH, SEQ, D_MODEL), 1.0),
        'x_sample': nrm(ks[1], (DEC_BATCH, DEC_SEQ, D_MODEL), 1.0),
        'meta_tokens': nrm(ks[2], (N_META, D_MODEL), 1.0),
        'ln_in_g': 1.0 + nrm(ks[3], (D_MODEL,), 0.02),
        'ln_in_b': nrm(ks[4], (D_MODEL,), 0.02),
        'w_in': nrm(ks[5], (DEPTH, D_MODEL, P_IN), D_MODEL ** -0.5) * jnp.asarray(col_scale),
        'b_gates': nrm(ks[6], (DEPTH, 4 * NH_A), 0.1) + jnp.asarray(fgate_bias),
        'conv_w': nrm(ks[7], (DEPTH, CONV_W, 2 * QK_A), CONV_W ** -0.5),
        'conv_b': nrm(ks[8], (DEPTH, 2 * QK_A), 0.02),
        'head_g': 1.0 + nrm(ks[9], (DEPTH, D_A), 0.02),
        'rpb': nrm(ks[10], (DEPTH, NH_B, 2 * KR_MAX - 1, 2 * KC - 1), 0.02),
        'w_a': nrm(ks[11], (DEPTH, D_A, D_MODEL), D_A ** -0.5),
        'w_b': nrm(ks[12], (DEPTH, D_B, D_MODEL), D_B ** -0.5),
        'w_out': nrm(ks[13], (DEPTH, D_MODEL, D_MODEL), BETA * D_MODEL ** -0.5),
        'ln1_g': 1.0 + nrm(ks[14], (DEPTH, D_MODEL), 0.02),
        'ln1_b': nrm(ks[15], (DEPTH, D_MODEL), 0.02),
        'router_g_w': nrm(ks[16], (DEPTH, D_MODEL, N_GROUPS), D_MODEL ** -0.5),
        'router_g_b': nrm(ks[17], (DEPTH, N_GROUPS), 0.01),
        'router_e_w': nrm(ks[18], (DEPTH, D_MODEL, N_EXPERTS), D_MODEL ** -0.5),
        'router_e_b': nrm(ks[19], (DEPTH, N_EXPERTS), 0.01),
        'w_gate': nrm(ks[20], (DEPTH, N_EXPERTS, D_MODEL, D_EXPERT), D_MODEL ** -0.5),
        'w_up': nrm(ks[21], (DEPTH, N_EXPERTS, D_MODEL, D_EXPERT), D_MODEL ** -0.5),
        'w_down': nrm(ks[22], (DEPTH, N_EXPERTS, D_EXPERT, D_MODEL), BETA * D_EXPERT ** -0.5),
        'ln2_g': 1.0 + nrm(ks[23], (DEPTH, D_MODEL), 0.02),
        'ln2_b': nrm(ks[24], (DEPTH, D_MODEL), 0.02),
    }


def reference(x_prompt, x_sample, meta_tokens, ln_in_g, ln_in_b, w_in, b_gates, conv_w, conv_b, head_g,
              rpb, w_a, w_b, w_out, ln1_g, ln1_b, router_g_w, router_g_b, router_e_w, router_e_b,
              w_gate, w_up, w_down, ln2_g, ln2_b):
    def encode(x):
        B = x.shape[0]
        meta = jnp.broadcast_to(meta_tokens.astype(x.dtype)[None], (B, N_META, D_MODEL))
        h = layer_norm(jnp.concatenate([meta, x], axis=1), ln_in_g, ln_in_b)
        for l in range(DEPTH):
            mix = token_mixer(h, w_in[l], b_gates[l], conv_w[l], conv_b[l], head_g[l], rpb[l],
                              w_a[l], w_b[l], w_out[l])
            h = layer_norm(ALPHA * h + mix, ln1_g[l], ln1_b[l])
            ffn = hier_moe(h, router_g_w[l], router_g_b[l], router_e_w[l], router_e_b[l],
                           w_gate[l], w_up[l], w_down[l])
            h = layer_norm(ALPHA * h + ffn, ln2_g[l], ln2_b[l])
        return h[:, N_META:]

    y_prompt = encode(x_prompt)
    y_sample = encode(x_sample)
    return (y_prompt, y_sample)
```

```python
import functools

import jax
import jax.numpy as jnp
import numpy as np
from jax import lax
from jax.experimental import pallas as pl
from jax.experimental.pallas import tpu as pltpu

F32 = jnp.float32
BF16 = jnp.bfloat16

D_MODEL = 2048
N_META = 16
GRID_W = 64
NH_A = 8
DK_A = 128
DV_A = 256
QK_A = NH_A * DK_A
D_A = NH_A * DV_A
NH_B = 16
DH_B = 64
D_B = NH_B * DH_B
KR_MAX = 8
KC = 16
N_GROUPS = 4
EXPERTS_PER_GROUP = 8
N_EXPERTS = N_GROUPS * EXPERTS_PER_GROUP
D_EXPERT = 1024
DEPTH = 1
ALPHA = (2 * DEPTH) ** 0.25
LN_EPS = 1e-5
NEG_INF = -1e30
IN_SPLITS = (QK_A, QK_A, D_A, D_A, 4 * NH_A, D_B, D_B, D_B, D_MODEL, D_MODEL)

OFF_QK = 0
OFF_VA = 2048
OFF_OA = 4096
OFF_GA = 6144
OFF_GB = 8192
OFF_QB = 10240
OFF_KB = 11264
OFF_VB = 12288
P_MAIN = 13312
LANES = 128

VMEM_LIMIT = 56 * 1024 * 1024

INPROJ_TM = 512
INPROJ_TN = 1024
CONV_TC = 512
MLSTM_CHUNK = 128
NAT_ROWS = 8
NAT_KROWS = 16
MERGE_TM = 256
OUT_TM = 256
MOE_BM = 256
COMB_TM = 256


def _cparams(sem):
    return pltpu.CompilerParams(dimension_semantics=sem, vmem_limit_bytes=VMEM_LIMIT)


def _layer_norm(x, g, b):
    xc = x - jnp.mean(x, -1, keepdims=True)
    var = jnp.mean(xc * xc, -1, keepdims=True)
    return xc * lax.rsqrt(var + LN_EPS) * g + b


def _inproj_kernel(x_ref, g_ref, b_ref, w_ref, wg_ref, o_ref, og_ref, h_scr):
    @pl.when(pl.program_id(1) == 0)
    def _():
        h = _layer_norm(x_ref[...], g_ref[...], b_ref[...]).astype(BF16)
        h_scr[...] = h
        og_ref[...] = jnp.dot(h, wg_ref[...], preferred_element_type=F32)

    o_ref[...] = jnp.dot(h_scr[...], w_ref[...], preferred_element_type=F32)


def _inproj(x, ln_g, ln_b, w_main, w_gates):
    n = x.shape[0]
    tm = min(INPROJ_TM, n)
    tn = INPROJ_TN
    return pl.pallas_call(
        _inproj_kernel,
        grid=(n // tm, P_MAIN // tn),
        in_specs=[
            pl.BlockSpec((tm, D_MODEL), lambda i, j: (i, 0)),
            pl.BlockSpec((1, D_MODEL), lambda i, j: (0, 0)),
            pl.BlockSpec((1, D_MODEL), lambda i, j: (0, 0)),
            pl.BlockSpec((D_MODEL, tn), lambda i, j: (0, j)),
            pl.BlockSpec((D_MODEL, LANES), lambda i, j: (0, 0)),
        ],
        out_specs=[
            pl.BlockSpec((tm, tn), lambda i, j: (i, j)),
            pl.BlockSpec((tm, LANES), lambda i, j: (i, 0)),
        ],
        out_shape=[
            jax.ShapeDtypeStruct((n, P_MAIN), F32),
            jax.ShapeDtypeStruct((n, LANES), F32),
        ],
        scratch_shapes=[pltpu.VMEM((tm, D_MODEL), BF16)],
        compiler_params=_cparams(("arbitrary", "arbitrary")),
        name="inproj",
    )(x, ln_g, ln_b, w_main, w_gates)


def _conv_kernel(x_ref, prev_ref, next_ref, meta_ref, w_ref, b_ref, qk_ref, kmeta_ref):
    t = pl.program_id(1)
    last = pl.num_programs(1) - 1
    tc = x_ref.shape[1]
    w0, w1, w2 = w_ref[0:1, :], w_ref[1:2, :], w_ref[2:3, :]
    bias = b_ref[...]
    lane = lax.broadcasted_iota(jnp.int32, (1, 2 * QK_A), 1)
    out_scale = jnp.where(lane >= QK_A, DK_A ** -0.5, 1.0).astype(F32)

    x = x_ref[0]
    meta = meta_ref[...]
    prev_row = jnp.where(t == 0, meta[N_META - 1:N_META, :], prev_ref[0, 7:8, :])
    next_row = jnp.where(t == last, jnp.zeros_like(prev_row), next_ref[0, 0:1, :])
    row = lax.broadcasted_iota(jnp.int32, (tc, 1), 0)
    x_prev = jnp.where(row == 0, prev_row, pltpu.roll(x, 1, 0))
    x_next = jnp.where(row == tc - 1, next_row, pltpu.roll(x, tc - 1, 0))
    y = x_prev * w0 + x * w1 + x_next * w2 + bias
    qk_ref[0] = (jax.nn.silu(y) * out_scale).astype(BF16)

    @pl.when(t == 0)
    def _():
        mrow = lax.broadcasted_iota(jnp.int32, (N_META, 1), 0)
        m_prev = jnp.where(mrow == 0, 0.0, pltpu.roll(meta, 1, 0))
        m_next = jnp.where(mrow == N_META - 1, x[0:1, :], pltpu.roll(meta, N_META - 1, 0))
        ym = m_prev * w0 + meta * w1 + m_next * w2 + bias
        km = jax.nn.silu(ym) * out_scale
        kmeta_ref[0] = km[:, QK_A:].astype(BF16)


def _conv(proj3, proj_meta, conv_w, conv_b):
    bsz, t_len, _ = proj3.shape
    tc = CONV_TC
    nt = t_len // tc
    r8 = tc // 8
    return pl.pallas_call(
        _conv_kernel,
        grid=(bsz, nt),
        in_specs=[
            pl.BlockSpec((1, tc, 2 * QK_A), lambda b, t: (b, t, 0)),
            pl.BlockSpec((1, 8, 2 * QK_A), lambda b, t: (b, jnp.maximum(t * r8 - 1, 0), 0)),
            pl.BlockSpec((1, 8, 2 * QK_A), lambda b, t: (b, jnp.minimum((t + 1) * r8, t_len // 8 - 1), 0)),
            pl.BlockSpec((N_META, 2 * QK_A), lambda b, t: (0, 0)),
            pl.BlockSpec((3, 2 * QK_A), lambda b, t: (0, 0)),
            pl.BlockSpec((1, 2 * QK_A), lambda b, t: (0, 0)),
        ],
        out_specs=[
            pl.BlockSpec((1, tc, 2 * QK_A), lambda b, t: (b, t, 0)),
            pl.BlockSpec((1, N_META, QK_A), lambda b, t: (b, 0, 0)),
        ],
        out_shape=[
            jax.ShapeDtypeStruct((bsz, t_len, 2 * QK_A), BF16),
            jax.ShapeDtypeStruct((bsz, N_META, QK_A), BF16),
        ],
        compiler_params=_cparams(("arbitrary", "arbitrary")),
        name="conv_silu",
    )(proj3, proj3, proj3, proj_meta, conv_w, conv_b)


def _split3(x):
    hi = x.astype(BF16)
    r1 = x - hi.astype(F32)
    mid = r1.astype(BF16)
    lo = (r1 - mid.astype(F32)).astype(BF16)
    return hi, mid, lo


def _tri_left(tri, x):
    hi, mid, lo = _split3(x)
    d = functools.partial(jnp.dot, preferred_element_type=F32)
    return d(tri, hi) + d(tri, mid) + d(tri, lo)


def _tri_right(x, tri):
    hi, mid, lo = _split3(x)
    d = functools.partial(jnp.dot, preferred_element_type=F32)
    return d(hi, tri) + d(mid, tri) + d(lo, tri)


def _state_update(k, v, ig_col, b_col, b_last, c_old, n_old, m_old):
    logw = b_last - b_col + ig_col
    m_new = jnp.maximum(b_last + m_old, jnp.max(logw, axis=0, keepdims=True))
    w = jnp.exp(logw - m_new)
    decay = jnp.exp(b_last + m_old - m_new)
    kw = k.astype(F32) * w
    kv = lax.dot_general(kw.astype(BF16), v, (((0,), (0,)), ((), ())), preferred_element_type=F32)
    c_new = decay * c_old + kv
    n_new = decay * n_old + jnp.sum(kw, axis=0, keepdims=True)
    return c_new, n_new, m_new


def _mlstm_kernel(qkf_ref, qkb_ref, vf_ref, vb_ref, gcf_ref, gcb_ref, grf_ref, grb_ref,
                  kmeta_ref, vmeta_ref, gmeta_ref, bcol_ref, brow_ref,
                  hf_ref, hb_ref, c_scr, n_scr, m_scr):
    j = pl.program_id(1)
    c = qkf_ref.shape[1]
    r_i = lax.broadcasted_iota(jnp.int32, (c, c), 0)
    c_i = lax.broadcasted_iota(jnp.int32, (c, c), 1)
    lower = r_i >= c_i
    upper = r_i <= c_i
    tri_l = lower.astype(BF16)
    tri_u = upper.astype(BF16)

    @pl.when(j == 0)
    def _():
        gm = gmeta_ref[...] + bcol_ref[...]
        lfm = jax.nn.log_sigmoid(gm)
        mr = lax.broadcasted_iota(jnp.int32, (N_META, N_META), 0)
        mc = lax.broadcasted_iota(jnp.int32, (N_META, N_META), 1)
        bm = _tri_left((mr >= mc).astype(BF16), lfm)
        zero_m = jnp.zeros((1, 1), F32)
        for hd in range(NH_A):
            ig_col = gm[:, hd:hd + 1]
            b_col = bm[:, 2 * NH_A + hd:2 * NH_A + hd + 1]
            b_last = b_col[N_META - 1:N_META, :]
            k = kmeta_ref[0, :, hd * DK_A:(hd + 1) * DK_A]
            v = vmeta_ref[:, hd * DV_A:(hd + 1) * DV_A].astype(BF16)
            c_new, n_new, m_new = _state_update(
                k, v, ig_col, b_col, b_last,
                jnp.zeros((DK_A, DV_A), F32), jnp.zeros((1, DK_A), F32), zero_m)
            c_scr[hd] = c_new
            n_scr[hd:hd + 1, :] = n_new
            m_scr[hd:hd + 1, :] = jnp.broadcast_to(m_new, (1, LANES))
        for hd in range(NH_A):
            u = NH_A + hd
            c_scr[u] = jnp.zeros((DK_A, DV_A), F32)
            n_scr[u:u + 1, :] = jnp.zeros((1, DK_A), F32)
            m_scr[u:u + 1, :] = jnp.zeros((1, LANES), F32)

    for direction in range(2):
        rev = direction == 1
        qk_ref = qkb_ref if rev else qkf_ref
        v_ref = vb_ref if rev else vf_ref
        gc = (gcb_ref if rev else gcf_ref)[0] + bcol_ref[...]
        gr = (grb_ref if rev else grf_ref)[0] + brow_ref[...]
        lf_c = jax.nn.log_sigmoid(gc)
        lf_r = jax.nn.log_sigmoid(gr)
        if rev:
            bc_all = _tri_left(tri_u, lf_c)
            br_all = _tri_right(lf_r, tri_l)
            mask = upper
        else:
            bc_all = _tri_left(tri_l, lf_c)
            br_all = _tri_right(lf_r, tri_u)
            mask = lower
        out_ref = hb_ref if rev else hf_ref
        for hd in range(NH_A):
            u = direction * NH_A + hd
            ig_lane = direction * NH_A + hd
            lf_lane = 2 * NH_A + direction * NH_A + hd
            ig_col = gc[:, ig_lane:ig_lane + 1]
            b_col = bc_all[:, lf_lane:lf_lane + 1]
            ig_row = gr[ig_lane:ig_lane + 1, :]
            b_row = br_all[lf_lane:lf_lane + 1, :]
            q = qk_ref[0, :, hd * DK_A:(hd + 1) * DK_A]
            k = qk_ref[0, :, QK_A + hd * DK_A:QK_A + (hd + 1) * DK_A]
            v = v_ref[0, :, hd * DV_A:(hd + 1) * DV_A].astype(BF16)
            c_old = c_scr[u]
            n_old = n_scr[u:u + 1, :]
            m_old = m_scr[u:u + 1, 0:1]

            logd = jnp.where(mask, b_col + (ig_row - b_row), NEG_INF)
            log_inter = b_col + m_old
            m_t = jnp.maximum(log_inter, jnp.max(logd, axis=-1, keepdims=True))
            s = lax.dot_general(q, k, (((1,), (1,)), ((), ())), preferred_element_type=F32)
            s = s * jnp.exp(logd - m_t)
            w_inter = jnp.exp(log_inter - m_t)
            q_c = jnp.dot(q, c_old.astype(BF16), preferred_element_type=F32)
            q_n = jnp.sum(q.astype(F32) * n_old, axis=-1, keepdims=True)
            num = jnp.dot(s.astype(BF16), v, preferred_element_type=F32) + w_inter * q_c
            den = jnp.sum(s, axis=-1, keepdims=True) + w_inter * q_n
            out_ref[0, :, hd * DV_A:(hd + 1) * DV_A] = num / jnp.maximum(jnp.abs(den), jnp.exp(-m_t))

            b_last = b_col[0:1, :] if rev else b_col[c - 1:c, :]
            c_new, n_new, m_new = _state_update(k, v, ig_col, b_col, b_last, c_old, n_old, m_old)
            c_scr[u] = c_new
            n_scr[u:u + 1, :] = n_new
            m_scr[u:u + 1, :] = jnp.broadcast_to(m_new, (1, LANES))


def _mlstm(qk, proj3, gates_col, gates_row, k_meta, proj_meta, gates_meta, bias_col, bias_row):
    bsz, t_len, _ = qk.shape
    c = MLSTM_CHUNK
    nc = t_len // c
    va_blk = OFF_VA // D_A
    fwd = lambda b, j: (b, j, 0)
    bwd = lambda b, j: (b, nc - 1 - j, 0)
    return pl.pallas_call(
        _mlstm_kernel,
        grid=(bsz, nc),
        in_specs=[
            pl.BlockSpec((1, c, 2 * QK_A), fwd),
            pl.BlockSpec((1, c, 2 * QK_A), bwd),
            pl.BlockSpec((1, c, D_A), lambda b, j: (b, j, va_blk)),
            pl.BlockSpec((1, c, D_A), lambda b, j: (b, nc - 1 - j, va_blk)),
            pl.BlockSpec((1, c, LANES), fwd),
            pl.BlockSpec((1, c, LANES), bwd),
            pl.BlockSpec((1, 4 * NH_A, c), lambda b, j: (b, 0, j)),
            pl.BlockSpec((1, 4 * NH_A, c), lambda b, j: (b, 0, nc - 1 - j)),
            pl.BlockSpec((1, N_META, QK_A), lambda b, j: (b, 0, 0)),
            pl.BlockSpec((N_META, D_A), lambda b, j: (0, va_blk)),
            pl.BlockSpec((N_META, LANES), lambda b, j: (0, 0)),
            pl.BlockSpec((1, LANES), lambda b, j: (0, 0)),
            pl.BlockSpec((4 * NH_A, 1), lambda b, j: (0, 0)),
        ],
        out_specs=[
            pl.BlockSpec((1, c, D_A), fwd),
            pl.BlockSpec((1, c, D_A), bwd),
        ],
        out_shape=[
            jax.ShapeDtypeStruct((bsz, t_len, D_A), F32),
            jax.ShapeDtypeStruct((bsz, t_len, D_A), F32),
        ],
        scratch_shapes=[
            pltpu.VMEM((2 * NH_A, DK_A, DV_A), F32),
            pltpu.VMEM((2 * NH_A, DK_A), F32),
            pltpu.VMEM((2 * NH_A, LANES), F32),
        ],
        compiler_params=_cparams(("arbitrary", "arbitrary")),
        name="mlstm",
    )(qk, qk, proj3, proj3, gates_col, gates_col, gates_row, gates_row,
      k_meta, proj_meta, gates_meta, bias_col, bias_row)


def _natten_bias_table(rpb):
    cols = np.arange(GRID_W)
    cstart = np.clip(cols - KC // 2, 0, GRID_W - KC)
    kc = cols[None, :]
    col_valid = (kc >= cstart[:, None]) & (kc < cstart[:, None] + KC)
    col_idx = np.clip(kc - cols[:, None] + KC - 1, 0, 2 * KC - 2)
    qr = np.arange(NAT_ROWS)[:, None]
    kk = np.arange(NAT_KROWS)[None, :]
    half = KR_MAX // 2
    rel_start = np.stack([np.maximum(qr - half, 0) + 0 * kk,
                          qr + 0 * kk,
                          np.minimum(qr + half, NAT_KROWS - KR_MAX) + 0 * kk])
    row_delta = np.stack([kk - qr, kk - qr - half, kk - qr - (NAT_KROWS - NAT_ROWS)])
    row_valid = (kk[None] >= rel_start) & (kk[None] < rel_start + KR_MAX)
    row_idx = np.clip(row_delta + KR_MAX - 1, 0, 2 * KR_MAX - 2)
    rpb_c = jnp.where(col_valid[None, None], rpb[:, :, col_idx], NEG_INF)
    tab = rpb_c[:, row_idx]
    tab = jnp.where(row_valid[None, :, :, :, None, None], tab, NEG_INF)
    tab = tab.transpose(1, 0, 2, 4, 3, 5)
    return tab.reshape(3, NH_B, NAT_ROWS * GRID_W, NAT_KROWS * GRID_W)


def _natten_kernel(q_ref, k_ref, v_ref, km_ref, vm_ref, bias_ref, o_ref):
    i = pl.program_id(2)
    rows = k_ref.shape[1] // GRID_W
    ks = jnp.clip(i * NAT_ROWS - KR_MAX // 2, 0, rows - NAT_KROWS)
    start = pl.multiple_of(ks * GRID_W, GRID_W * 4)
    nk = NAT_KROWS * GRID_W
    lane = lax.broadcasted_iota(jnp.int32, (1, LANES), 1)
    q = (q_ref[0] * (DH_B ** -0.5)).astype(BF16)
    kwin = k_ref[0, pl.ds(start, nk), :]
    vwin = v_ref[0, pl.ds(start, nk), :].astype(BF16)
    kmeta = km_ref[...]
    vmeta = vm_ref[...].astype(BF16)
    nt = (((1,), (1,)), ((), ()))
    out = jnp.zeros(q_ref.shape[1:], F32)
    for h in range(2):
        sel = (lane >= h * DH_B) & (lane < (h + 1) * DH_B)
        kh = jnp.where(sel, kwin, 0.0).astype(BF16)
        kmh = jnp.where(sel, kmeta, 0.0).astype(BF16)
        s = lax.dot_general(q, kh, nt, preferred_element_type=F32) + bias_ref[0, h]
        sm = lax.dot_general(q, kmh, nt, preferred_element_type=F32)
        mx = jnp.maximum(jnp.max(s, -1, keepdims=True), jnp.max(sm, -1, keepdims=True))
        p = jnp.exp(s - mx)
        pm = jnp.exp(sm - mx)
        denom = jnp.sum(p, -1, keepdims=True) + jnp.sum(pm, -1, keepdims=True)
        o = (jnp.dot(p.astype(BF16), vwin, preferred_element_type=F32)
             + jnp.dot(pm.astype(BF16), vmeta, preferred_element_type=F32)) / denom
        out = jnp.where(sel, o, out)
    o_ref[0] = out


def _natten(proj3, proj_meta, bias_tab):
    bsz, t_len, _ = proj3.shape
    rows = t_len // GRID_W
    assert rows % NAT_ROWS == 0 and rows >= NAT_KROWS
    nb = rows // NAT_ROWS
    nq = NAT_ROWS * GRID_W
    hp = NH_B // 2
    qb, kb, vb = OFF_QB // LANES, OFF_KB // LANES, OFF_VB // LANES

    def pat(i):
        return jnp.where(i == 0, 0, jnp.where(i == nb - 1, 2, 1))

    return pl.pallas_call(
        _natten_kernel,
        grid=(bsz, hp, nb),
        in_specs=[
            pl.BlockSpec((1, nq, LANES), lambda b, p, i: (b, i, qb + p)),
            pl.BlockSpec((1, t_len, LANES), lambda b, p, i: (b, 0, kb + p)),
            pl.BlockSpec((1, t_len, LANES), lambda b, p, i: (b, 0, vb + p)),
            pl.BlockSpec((N_META, LANES), lambda b, p, i: (0, kb + p)),
            pl.BlockSpec((N_META, LANES), lambda b, p, i: (0, vb + p)),
            pl.BlockSpec((1, 2, nq, NAT_KROWS * GRID_W), lambda b, p, i: (pat(i), p, 0, 0)),
        ],
        out_specs=pl.BlockSpec((1, nq, LANES), lambda b, p, i: (b, i, p)),
        out_shape=jax.ShapeDtypeStruct((bsz, t_len, D_B), F32),
        compiler_params=_cparams(("arbitrary", "arbitrary", "arbitrary")),
        name="natten",
    )(proj3, proj3, proj3, proj_meta, proj_meta, bias_tab)


def _merge_kernel(hf_ref, hb_ref, oa_ref, ga_ref, gb_ref, yb_ref, hg_ref, wa_ref, wb_ref, o_ref, ya_scr):
    for hd in range(NH_A):
        sl = slice(hd * DV_A, (hd + 1) * DV_A)
        h = hf_ref[:, sl] + hb_ref[:, sl]
        hc = h - jnp.mean(h, -1, keepdims=True)
        var = jnp.mean(hc * hc, -1, keepdims=True)
        hn = hc * lax.rsqrt(var + LN_EPS) * hg_ref[:, sl]
        ya_scr[:, sl] = (hn * jax.nn.sigmoid(oa_ref[:, sl])).astype(BF16)
    a = jnp.dot(ya_scr[...], wa_ref[...], preferred_element_type=F32)
    b = jnp.dot(yb_ref[...].astype(BF16), wb_ref[...], preferred_element_type=F32)
    o_ref[...] = (jax.nn.sigmoid(ga_ref[...]) * a + jax.nn.sigmoid(gb_ref[...]) * b).astype(BF16)


def _merge(h_f, h_b, proj, y_b, head_g, w_a, w_b):
    n = h_f.shape[0]
    tm = MERGE_TM
    const = lambda i: (0, 0)
    return pl.pallas_call(
        _merge_kernel,
        grid=(n // tm,),
        in_specs=[
            pl.BlockSpec((tm, D_A), lambda i: (i, 0)),
            pl.BlockSpec((tm, D_A), lambda i: (i, 0)),
            pl.BlockSpec((tm, D_A), lambda i: (i, OFF_OA // D_A)),
            pl.BlockSpec((tm, D_MODEL), lambda i: (i, OFF_GA // D_MODEL)),
            pl.BlockSpec((tm, D_MODEL), lambda i: (i, OFF_GB // D_MODEL)),
            pl.BlockSpec((tm, D_B), lambda i: (i, 0)),
            pl.BlockSpec((1, D_A), const),
            pl.BlockSpec((D_A, D_MODEL), const, pipeline_mode=pl.Buffered(1)),
            pl.BlockSpec((D_B, D_MODEL), const, pipeline_mode=pl.Buffered(1)),
        ],
        out_specs=pl.BlockSpec((tm, D_MODEL), lambda i: (i, 0)),
        out_shape=jax.ShapeDtypeStruct((n, D_MODEL), BF16),
        scratch_shapes=[pltpu.VMEM((tm, D_A), BF16)],
        compiler_params=_cparams(("arbitrary",)),
        name="merge",
    )(h_f, h_b, proj, proj, proj, y_b, head_g, w_a, w_b)


def _outproj_router_kernel(m_ref, x_ref, lg_ref, lb_ref, wo_ref, g1_ref, b1_ref, wr_hi_ref, wr_lo_ref, br_ref,
                           h1_ref, route_ref):
    mix = jnp.dot(m_ref[...], wo_ref[...], preferred_element_type=F32)
    h0 = _layer_norm(x_ref[...], lg_ref[...], lb_ref[...])
    h1 = _layer_norm(ALPHA * h0 + mix, g1_ref[...], b1_ref[...])
    h1_ref[...] = h1

    hi = h1.astype(BF16)
    lo = (h1 - hi.astype(F32)).astype(BF16)
    d = functools.partial(jnp.dot, preferred_element_type=F32)
    logits = d(hi, wr_hi_ref[...]) + (d(lo, wr_hi_ref[...]) + d(hi, wr_lo_ref[...])) + br_ref[...]

    lane = lax.broadcasted_iota(jnp.int32, logits.shape, 1)
    big = jnp.int32(LANES)
    is_g = lane < N_GROUPS
    gl = jnp.where(is_g, logits, NEG_INF)
    gmax = jnp.max(gl, -1, keepdims=True)
    gsel = jnp.min(jnp.where(is_g & (gl == gmax), lane, big), -1, keepdims=True)
    pg = 1.0 / jnp.sum(jnp.where(is_g, jnp.exp(gl - gmax), 0.0), -1, keepdims=True)
    e_lo = N_GROUPS + gsel * EXPERTS_PER_GROUP
    in_grp = (lane >= e_lo) & (lane < e_lo + EXPERTS_PER_GROUP)
    el = jnp.where(in_grp, logits, NEG_INF)
    v0 = jnp.max(el, -1, keepdims=True)
    i0 = jnp.min(jnp.where(in_grp & (el == v0), lane, big), -1, keepdims=True)
    el1 = jnp.where(lane == i0, NEG_INF, el)
    v1 = jnp.max(el1, -1, keepdims=True)
    i1 = jnp.min(jnp.where(in_grp & (lane != i0) & (el1 == v1), lane, big), -1, keepdims=True)
    e1 = jnp.exp(v1 - v0)
    den = 1.0 + e1
    g0 = pg * (1.0 / den)
    g1 = pg * (e1 / den)
    out = jnp.where(lane == 0, (i0 - N_GROUPS).astype(F32),
                    jnp.where(lane == 1, (i1 - N_GROUPS).astype(F32),
                              jnp.where(lane == 2, g0, jnp.where(lane == 3, g1, 0.0))))
    route_ref[...] = out


def _outproj_router(merged, x, ln_in_g, ln_in_b, w_out, ln1_g, ln1_b, wr_hi, wr_lo, br):
    n = x.shape[0]
    tm = OUT_TM
    const = lambda i: (0, 0)
    row = lambda i: (i, 0)
    return pl.pallas_call(
        _outproj_router_kernel,
        grid=(n // tm,),
        in_specs=[
            pl.BlockSpec((tm, D_MODEL), row),
            pl.BlockSpec((tm, D_MODEL), row),
            pl.BlockSpec((1, D_MODEL), const),
            pl.BlockSpec((1, D_MODEL), const),
            pl.BlockSpec((D_MODEL, D_MODEL), const, pipeline_mode=pl.Buffered(1)),
            pl.BlockSpec((1, D_MODEL), const),
            pl.BlockSpec((1, D_MODEL), const),
            pl.BlockSpec((D_MODEL, LANES), const),
            pl.BlockSpec((D_MODEL, LANES), const),
            pl.BlockSpec((1, LANES), const),
        ],
        out_specs=[pl.BlockSpec((tm, D_MODEL), row), pl.BlockSpec((tm, LANES), row)],
        out_shape=[jax.ShapeDtypeStruct((n, D_MODEL), F32), jax.ShapeDtypeStruct((n, LANES), F32)],
        compiler_params=_cparams(("arbitrary",)),
        name="outproj_router",
    )(merged, x, ln_in_g, ln_in_b, w_out, ln1_g, ln1_b, wr_hi, wr_lo, br)


def _moe_ffn_kernel(be_ref, nu_ref, tok_ref, h1_hbm, wg_ref, wu_ref, wd_ref, y_ref, xbuf, sem):
    i = pl.program_id(0)
    bm = xbuf.shape[0]

    @pl.when(i < nu_ref[0])
    def _():
        def issue(r, carry):
            t = tok_ref[0, 0, r]
            pltpu.make_async_copy(h1_hbm.at[pl.ds(t, 1)], xbuf.at[pl.ds(r, 1)], sem).start()
            return carry

        lax.fori_loop(0, bm, issue, 0, unroll=8)
        pltpu.make_async_copy(h1_hbm.at[pl.ds(0, bm)], xbuf, sem).wait()
        xb = xbuf[...].astype(BF16)
        g = jnp.dot(xb, wg_ref[0], preferred_element_type=F32)
        u = jnp.dot(xb, wu_ref[0], preferred_element_type=F32)
        hb = (jax.nn.silu(g) * u).astype(BF16)
        y_ref[...] = jnp.dot(hb, wd_ref[0], preferred_element_type=F32)

    @pl.when(i >= nu_ref[0])
    def _():
        y_ref[...] = jnp.zeros(y_ref.shape, F32)


def _moe_ffn(block_e, n_used, tok3, h1, w_gate, w_up, w_down):
    nb = tok3.shape[0]
    bm = MOE_BM
    wmap = lambda i, be, nu: (be[i], 0, 0)
    grid_spec = pltpu.PrefetchScalarGridSpec(
        num_scalar_prefetch=2,
        grid=(nb,),
        in_specs=[
            pl.BlockSpec((1, 1, bm), lambda i, be, nu: (i, 0, 0), memory_space=pltpu.SMEM),
            pl.BlockSpec(memory_space=pl.ANY),
            pl.BlockSpec((1, D_MODEL, D_EXPERT), wmap),
            pl.BlockSpec((1, D_MODEL, D_EXPERT), wmap),
            pl.BlockSpec((1, D_EXPERT, D_MODEL), wmap),
        ],
        out_specs=pl.BlockSpec((bm, D_MODEL), lambda i, be, nu: (i, 0)),
        scratch_shapes=[pltpu.VMEM((bm, D_MODEL), F32), pltpu.SemaphoreType.DMA(())],
    )
    return pl.pallas_call(
        _moe_ffn_kernel,
        grid_spec=grid_spec,
        out_shape=jax.ShapeDtypeStruct((nb * bm, D_MODEL), F32),
        compiler_params=_cparams(("arbitrary",)),
        name="moe_ffn",
    )(block_e, n_used, tok3, h1, w_gate, w_up, w_down)


def _combine_kernel(pos_ref, y_hbm, h1_ref, route_ref, g_ref, b_ref, o_ref, buf0, buf1, sem):
    tm = buf0.shape[0]

    def issue(r, carry):
        p0 = pos_ref[0, 0, 2 * r]
        p1 = pos_ref[0, 0, 2 * r + 1]
        pltpu.make_async_copy(y_hbm.at[pl.ds(p0, 1)], buf0.at[pl.ds(r, 1)], sem.at[0]).start()
        pltpu.make_async_copy(y_hbm.at[pl.ds(p1, 1)], buf1.at[pl.ds(r, 1)], sem.at[1]).start()
        return carry

    lax.fori_loop(0, tm, issue, 0, unroll=8)
    pltpu.make_async_copy(y_hbm.at[pl.ds(0, tm)], buf0, sem.at[0]).wait()
    pltpu.make_async_copy(y_hbm.at[pl.ds(0, tm)], buf1, sem.at[1]).wait()
    route = route_ref[...]
    ffn = buf0[...] * route[:, 2:3] + buf1[...] * route[:, 3:4]
    o_ref[...] = _layer_norm(ALPHA * h1_ref[...] + ffn, g_ref[...], b_ref[...])


def _combine(pos3, y_sorted, h1, route, ln2_g, ln2_b):
    n = h1.shape[0]
    tm = COMB_TM
    const = lambda i: (0, 0)
    row = lambda i: (i, 0)
    return pl.pallas_call(
        _combine_kernel,
        grid=(n // tm,),
        in_specs=[
            pl.BlockSpec((1, 1, 2 * tm), lambda i: (i, 0, 0), memory_space=pltpu.SMEM),
            pl.BlockSpec(memory_space=pl.ANY),
            pl.BlockSpec((tm, D_MODEL), row),
            pl.BlockSpec((tm, LANES), row),
            pl.BlockSpec((1, D_MODEL), const),
            pl.BlockSpec((1, D_MODEL), const),
        ],
        out_specs=pl.BlockSpec((tm, D_MODEL), row),
        out_shape=jax.ShapeDtypeStruct((n, D_MODEL), F32),
        scratch_shapes=[pltpu.VMEM((tm, D_MODEL), F32), pltpu.VMEM((tm, D_MODEL), F32),
                        pltpu.SemaphoreType.DMA((2,))],
        compiler_params=_cparams(("arbitrary",)),
        name="combine",
    )(pos3, y_sorted, h1, route, ln2_g, ln2_b)


def _dispatch_plan(route, n):
    bm = MOE_BM
    a = 2 * n
    nb = a // bm + N_EXPERTS
    flat_e = route[:, 0:2].astype(jnp.int32).reshape(a)
    onehot = (flat_e[:, None] == jnp.arange(N_EXPERTS, dtype=jnp.int32)[None, :]).astype(jnp.int32)
    csum = jnp.cumsum(onehot, axis=0)
    rank = jnp.take_along_axis(csum, flat_e[:, None], axis=1)[:, 0] - 1
    counts = csum[-1]
    pcounts = (counts + bm - 1) // bm * bm
    pend = jnp.cumsum(pcounts)
    pstart = pend - pcounts
    pos = pstart[flat_e] + rank
    tok = jnp.zeros((nb * bm,), jnp.int32).at[pos].set(jnp.arange(a, dtype=jnp.int32) // 2)
    block_e = jnp.minimum(
        jnp.searchsorted(pend, jnp.arange(nb, dtype=jnp.int32) * bm, side="right"), N_EXPERTS - 1
    ).astype(jnp.int32)
    n_used = (pend[-1:] // bm).astype(jnp.int32)
    return block_e, n_used, tok.reshape(nb, 1, bm), pos.reshape(n // COMB_TM, 1, 2 * COMB_TM)


def _encode(x, p):
    bsz, t_len, _ = x.shape
    n = bsz * t_len
    x2 = x.reshape(n, D_MODEL)
    proj, gates = _inproj(x2, p["ln_in_g"], p["ln_in_b"], p["w_main"], p["w_gates"])
    proj3 = proj.reshape(bsz, t_len, P_MAIN)
    gates3 = gates.reshape(bsz, t_len, LANES)
    gates_row = gates3[:, :, :4 * NH_A].transpose(0, 2, 1)
    qk, k_meta = _conv(proj3, p["proj_meta"], p["conv_w"], p["conv_b"])
    h_f, h_b = _mlstm(qk, proj3, gates3, gates_row, k_meta, p["proj_meta"], p["gates_meta"],
                      p["bias_col"], p["bias_row"])
    y_b = _natten(proj3, p["proj_meta"], p["bias_tab"])
    merged = _merge(h_f.reshape(n, D_A), h_b.reshape(n, D_A), proj, y_b.reshape(n, D_B),
                    p["head_g"], p["w_a"], p["w_b"])
    h1, route = _outproj_router(merged, x2, p["ln_in_g"], p["ln_in_b"], p["w_out"], p["ln1_g"], p["ln1_b"],
                                p["wr_hi"], p["wr_lo"], p["br"])
    block_e, n_used, tok3, pos3 = _dispatch_plan(route, n)
    y_sorted = _moe_ffn(block_e, n_used, tok3, h1, p["w_gate"], p["w_up"], p["w_down"])
    out = _combine(pos3, y_sorted, h1, route, p["ln2_g"], p["ln2_b"])
    return out.reshape(bsz, t_len, D_MODEL)


def kernel(x_prompt, x_sample, meta_tokens, ln_in_g, ln_in_b, w_in, b_gates, conv_w, conv_b, head_g, rpb, w_a, w_b, w_out, ln1_g, ln1_b, router_g_w, router_g_b, router_e_w, router_e_b, w_gate, w_up, w_down, ln2_g, ln2_b):
    l = 0
    offs = np.cumsum((0,) + IN_SPLITS)
    col = lambda i: w_in[l][:, offs[i]:offs[i + 1]]
    w_main = jnp.concatenate([col(0), col(1), col(2), col(3), col(8), col(9), col(5), col(6), col(7)],
                             axis=1).astype(BF16)
    w_gates = jnp.pad(col(4), ((0, 0), (0, LANES - 4 * NH_A))).astype(BF16)
    row = lambda v: v.reshape(1, -1).astype(F32)
    wr = jnp.pad(jnp.concatenate([router_g_w[l], router_e_w[l]], axis=1),
                 ((0, 0), (0, LANES - N_GROUPS - N_EXPERTS)))
    wr_hi = wr.astype(BF16)
    p = {
        "ln_in_g": row(ln_in_g), "ln_in_b": row(ln_in_b),
        "w_main": w_main, "w_gates": w_gates,
        "conv_w": conv_w[l], "conv_b": row(conv_b[l]),
        "bias_col": jnp.pad(b_gates[l], (0, LANES - 4 * NH_A)).reshape(1, LANES),
        "bias_row": b_gates[l].reshape(4 * NH_A, 1),
        "head_g": row(head_g[l]),
        "bias_tab": _natten_bias_table(rpb[l]),
        "w_a": w_a[l].astype(BF16), "w_b": w_b[l].astype(BF16), "w_out": w_out[l].astype(BF16),
        "ln1_g": row(ln1_g[l]), "ln1_b": row(ln1_b[l]),
        "wr_hi": wr_hi, "wr_lo": (wr - wr_hi.astype(F32)).astype(BF16),
        "br": jnp.pad(jnp.concatenate([router_g_b[l], router_e_b[l]]),
                      (0, LANES - N_GROUPS - N_EXPERTS)).reshape(1, LANES),
        "w_gate": w_gate[l].astype(BF16), "w_up": w_up[l].astype(BF16), "w_down": w_down[l].astype(BF16),
        "ln2_g": row(ln2_g[l]), "ln2_b": row(ln2_b[l]),
    }
    p["proj_meta"], p["gates_meta"] = _inproj(meta_tokens, p["ln_in_g"], p["ln_in_b"], w_main, w_gates)
    return (_encode(x_prompt, p), _encode(x_sample, p))
```

```python
import functools

import jax
import jax.numpy as jnp
import numpy as np
from jax import lax
from jax.experimental import pallas as pl
from jax.experimental.pallas import tpu as pltpu

F32 = jnp.float32
BF16 = jnp.bfloat16

D_MODEL = 2048
N_META = 16
GRID_W = 64
NH_A = 8
DK_A = 128
DV_A = 256
QK_A = NH_A * DK_A
D_A = NH_A * DV_A
NH_B = 16
DH_B = 64
D_B = NH_B * DH_B
KR_MAX = 8
KC = 16
N_GROUPS = 4
EXPERTS_PER_GROUP = 8
N_EXPERTS = N_GROUPS * EXPERTS_PER_GROUP
D_EXPERT = 1024
DEPTH = 1
ALPHA = (2 * DEPTH) ** 0.25
LN_EPS = 1e-5
NEG_INF = -1e30
IN_SPLITS = (QK_A, QK_A, D_A, D_A, 4 * NH_A, D_B, D_B, D_B, D_MODEL, D_MODEL)

OFF_QK = 0
OFF_VA = 2048
OFF_OA = 4096
OFF_GA = 6144
OFF_GB = 8192
OFF_QB = 10240
OFF_KB = 11264
OFF_VB = 12288
P_MAIN = 13312
LANES = 128

VMEM_LIMIT = 56 * 1024 * 1024

INPROJ_TM = 1024
INPROJ_TN = 1024
INPROJ_LN_SLAB = 256
CONV_TC = 512
MLSTM_CHUNK = 128
NAT_ROWS = 8
NAT_KROWS = 16
MERGE_TM = 256
OUT_TM = 256
MOE_BM = 256
MOE_NCHUNK = 4
COMB_TM = 256


def _cparams(sem):
    return pltpu.CompilerParams(dimension_semantics=sem, vmem_limit_bytes=VMEM_LIMIT)


def _layer_norm(x, g, b):
    xc = x - jnp.mean(x, -1, keepdims=True)
    var = jnp.mean(xc * xc, -1, keepdims=True)
    return xc * lax.rsqrt(var + LN_EPS) * g + b


def _inproj_kernel(x_ref, g_ref, b_ref, w_ref, wg_ref, o_ref, og_ref, h_scr):
    @pl.when(pl.program_id(1) == 0)
    def _():
        tm = x_ref.shape[0]
        slab = min(INPROJ_LN_SLAB, tm)

        def ln_slab(s, carry):
            r = pl.multiple_of(s * slab, slab)
            h = _layer_norm(x_ref[pl.ds(r, slab), :], g_ref[...], b_ref[...]).astype(BF16)
            h_scr[pl.ds(r, slab), :] = h
            og_ref[pl.ds(r, slab), :] = jnp.dot(h, wg_ref[...], preferred_element_type=F32)
            return carry

        lax.fori_loop(0, tm // slab, ln_slab, 0)

    o_ref[...] = jnp.dot(h_scr[...], w_ref[...], preferred_element_type=F32)


def _inproj(x, ln_g, ln_b, w_main, w_gates):
    n = x.shape[0]
    tm = min(INPROJ_TM, n)
    tn = INPROJ_TN
    return pl.pallas_call(
        _inproj_kernel,
        grid=(n // tm, P_MAIN // tn),
        in_specs=[
            pl.BlockSpec((tm, D_MODEL), lambda i, j: (i, 0)),
            pl.BlockSpec((1, D_MODEL), lambda i, j: (0, 0)),
            pl.BlockSpec((1, D_MODEL), lambda i, j: (0, 0)),
            pl.BlockSpec((D_MODEL, tn), lambda i, j: (0, j)),
            pl.BlockSpec((D_MODEL, LANES), lambda i, j: (0, 0)),
        ],
        out_specs=[
            pl.BlockSpec((tm, tn), lambda i, j: (i, j)),
            pl.BlockSpec((tm, LANES), lambda i, j: (i, 0)),
        ],
        out_shape=[
            jax.ShapeDtypeStruct((n, P_MAIN), F32),
            jax.ShapeDtypeStruct((n, LANES), F32),
        ],
        scratch_shapes=[pltpu.VMEM((tm, D_MODEL), BF16)],
        compiler_params=_cparams(("arbitrary", "arbitrary")),
        name="inproj",
    )(x, ln_g, ln_b, w_main, w_gates)


def _conv_kernel(x_ref, prev_ref, next_ref, meta_ref, w_ref, b_ref, qk_ref, kmeta_ref):
    t = pl.program_id(1)
    last = pl.num_programs(1) - 1
    tc = x_ref.shape[1]
    w0, w1, w2 = w_ref[0:1, :], w_ref[1:2, :], w_ref[2:3, :]
    bias = b_ref[...]
    lane = lax.broadcasted_iota(jnp.int32, (1, 2 * QK_A), 1)
    out_scale = jnp.where(lane >= QK_A, DK_A ** -0.5, 1.0).astype(F32)

    x = x_ref[0]
    meta = meta_ref[...]
    prev_row = jnp.where(t == 0, meta[N_META - 1:N_META, :], prev_ref[0, 7:8, :])
    next_row = jnp.where(t == last, jnp.zeros_like(prev_row), next_ref[0, 0:1, :])
    row = lax.broadcasted_iota(jnp.int32, (tc, 1), 0)
    x_prev = jnp.where(row == 0, prev_row, pltpu.roll(x, 1, 0))
    x_next = jnp.where(row == tc - 1, next_row, pltpu.roll(x, tc - 1, 0))
    y = x_prev * w0 + x * w1 + x_next * w2 + bias
    qk_ref[0] = (jax.nn.silu(y) * out_scale).astype(BF16)

    @pl.when(t == 0)
    def _():
        mrow = lax.broadcasted_iota(jnp.int32, (N_META, 1), 0)
        m_prev = jnp.where(mrow == 0, 0.0, pltpu.roll(meta, 1, 0))
        m_next = jnp.where(mrow == N_META - 1, x[0:1, :], pltpu.roll(meta, N_META - 1, 0))
        ym = m_prev * w0 + meta * w1 + m_next * w2 + bias
        km = jax.nn.silu(ym) * out_scale
        kmeta_ref[0] = km[:, QK_A:].astype(BF16)


def _conv(proj3, proj_meta, conv_w, conv_b):
    bsz, t_len, _ = proj3.shape
    tc = CONV_TC
    nt = t_len // tc
    r8 = tc // 8
    return pl.pallas_call(
        _conv_kernel,
        grid=(bsz, nt),
        in_specs=[
            pl.BlockSpec((1, tc, 2 * QK_A), lambda b, t: (b, t, 0)),
            pl.BlockSpec((1, 8, 2 * QK_A), lambda b, t: (b, jnp.maximum(t * r8 - 1, 0), 0)),
            pl.BlockSpec((1, 8, 2 * QK_A), lambda b, t: (b, jnp.minimum((t + 1) * r8, t_len // 8 - 1), 0)),
            pl.BlockSpec((N_META, 2 * QK_A), lambda b, t: (0, 0)),
            pl.BlockSpec((3, 2 * QK_A), lambda b, t: (0, 0)),
            pl.BlockSpec((1, 2 * QK_A), lambda b, t: (0, 0)),
        ],
        out_specs=[
            pl.BlockSpec((1, tc, 2 * QK_A), lambda b, t: (b, t, 0)),
            pl.BlockSpec((1, N_META, QK_A), lambda b, t: (b, 0, 0)),
        ],
        out_shape=[
            jax.ShapeDtypeStruct((bsz, t_len, 2 * QK_A), BF16),
            jax.ShapeDtypeStruct((bsz, N_META, QK_A), BF16),
        ],
        compiler_params=_cparams(("arbitrary", "arbitrary")),
        name="conv_silu",
    )(proj3, proj3, proj3, proj_meta, conv_w, conv_b)


def _split3(x):
    hi = x.astype(BF16)
    r1 = x - hi.astype(F32)
    mid = r1.astype(BF16)
    lo = (r1 - mid.astype(F32)).astype(BF16)
    return hi, mid, lo


def _tri_left(tri, x):
    hi, mid, lo = _split3(x)
    d = functools.partial(jnp.dot, preferred_element_type=F32)
    return d(tri, hi) + d(tri, mid) + d(tri, lo)


def _tri_right(x, tri):
    hi, mid, lo = _split3(x)
    d = functools.partial(jnp.dot, preferred_element_type=F32)
    return d(hi, tri) + d(mid, tri) + d(lo, tri)


def _state_update(k, v, ig_col, b_col, b_last, c_old, n_old, m_old):
    logw = b_last - b_col + ig_col
    m_new = jnp.maximum(b_last + m_old, jnp.max(logw, axis=0, keepdims=True))
    w = jnp.exp(logw - m_new)
    decay = jnp.exp(b_last + m_old - m_new)
    kw = k.astype(F32) * w
    kv = lax.dot_general(kw.astype(BF16), v, (((0,), (0,)), ((), ())), preferred_element_type=F32)
    c_new = decay * c_old + kv
    n_new = decay * n_old + jnp.sum(kw, axis=0, keepdims=True)
    return c_new, n_new, m_new


def _mlstm_kernel(qkf_ref, qkb_ref, vf_ref, vb_ref, gcf_ref, gcb_ref, grf_ref, grb_ref,
                  kmeta_ref, vmeta_ref, gmeta_ref, bcol_ref, brow_ref,
                  hf_ref, hb_ref, c_scr, n_scr, m_scr):
    j = pl.program_id(1)
    c = qkf_ref.shape[1]
    r_i = lax.broadcasted_iota(jnp.int32, (c, c), 0)
    c_i = lax.broadcasted_iota(jnp.int32, (c, c), 1)
    lower = r_i >= c_i
    upper = r_i <= c_i
    tri_l = lower.astype(BF16)
    tri_u = upper.astype(BF16)

    @pl.when(j == 0)
    def _():
        gm = gmeta_ref[...] + bcol_ref[...]
        lfm = jax.nn.log_sigmoid(gm)
        mr = lax.broadcasted_iota(jnp.int32, (N_META, N_META), 0)
        mc = lax.broadcasted_iota(jnp.int32, (N_META, N_META), 1)
        bm = _tri_left((mr >= mc).astype(BF16), lfm)
        zero_m = jnp.zeros((1, 1), F32)
        for hd in range(NH_A):
            ig_col = gm[:, hd:hd + 1]
            b_col = bm[:, 2 * NH_A + hd:2 * NH_A + hd + 1]
            b_last = b_col[N_META - 1:N_META, :]
            k = kmeta_ref[0, :, hd * DK_A:(hd + 1) * DK_A]
            v = vmeta_ref[:, hd * DV_A:(hd + 1) * DV_A].astype(BF16)
            c_new, n_new, m_new = _state_update(
                k, v, ig_col, b_col, b_last,
                jnp.zeros((DK_A, DV_A), F32), jnp.zeros((1, DK_A), F32), zero_m)
            c_scr[hd] = c_new
            n_scr[hd:hd + 1, :] = n_new
            m_scr[hd:hd + 1, :] = jnp.broadcast_to(m_new, (1, LANES))
        for hd in range(NH_A):
            u = NH_A + hd
            c_scr[u] = jnp.zeros((DK_A, DV_A), F32)
            n_scr[u:u + 1, :] = jnp.zeros((1, DK_A), F32)
            m_scr[u:u + 1, :] = jnp.zeros((1, LANES), F32)

    c_all = [c_scr[u] for u in range(2 * NH_A)]
    n_all = n_scr[...]
    m_all = m_scr[...]
    c_out, n_out, m_out = [], [], []
    for direction in range(2):
        rev = direction == 1
        qk_ref = qkb_ref if rev else qkf_ref
        v_ref = vb_ref if rev else vf_ref
        gc = (gcb_ref if rev else gcf_ref)[0] + bcol_ref[...]
        gr = (grb_ref if rev else grf_ref)[0] + brow_ref[...]
        lf_c = jax.nn.log_sigmoid(gc)
        lf_r = jax.nn.log_sigmoid(gr)
        if rev:
            bc_all = _tri_left(tri_u, lf_c)
            br_all = _tri_right(lf_r, tri_l)
            mask = upper
        else:
            bc_all = _tri_left(tri_l, lf_c)
            br_all = _tri_right(lf_r, tri_u)
            mask = lower
        out_ref = hb_ref if rev else hf_ref
        for hd in range(NH_A):
            u = direction * NH_A + hd
            ig_lane = direction * NH_A + hd
            lf_lane = 2 * NH_A + direction * NH_A + hd
            ig_col = gc[:, ig_lane:ig_lane + 1]
            b_col = bc_all[:, lf_lane:lf_lane + 1]
            ig_row = gr[ig_lane:ig_lane + 1, :]
            b_row = br_all[lf_lane:lf_lane + 1, :]
            q = qk_ref[0, :, hd * DK_A:(hd + 1) * DK_A]
            k = qk_ref[0, :, QK_A + hd * DK_A:QK_A + (hd + 1) * DK_A]
            v = v_ref[0, :, hd * DV_A:(hd + 1) * DV_A].astype(BF16)
            c_old = c_all[u]
            n_old = n_all[u:u + 1, :]
            m_old = m_all[u:u + 1, 0:1]

            logd = jnp.where(mask, b_col + (ig_row - b_row), NEG_INF)
            log_inter = b_col + m_old
            m_t = jnp.maximum(log_inter, jnp.max(logd, axis=-1, keepdims=True))
            s = lax.dot_general(q, k, (((1,), (1,)), ((), ())), preferred_element_type=F32)
            s = s * jnp.exp(logd - m_t)
            w_inter = jnp.exp(log_inter - m_t)
            q_c = jnp.dot(q, c_old.astype(BF16), preferred_element_type=F32)
            q_n = jnp.sum(q.astype(F32) * n_old, axis=-1, keepdims=True)
            num = jnp.dot(s.astype(BF16), v, preferred_element_type=F32) + w_inter * q_c
            den = jnp.sum(s, axis=-1, keepdims=True) + w_inter * q_n
            out_ref[0, :, hd * DV_A:(hd + 1) * DV_A] = num / jnp.maximum(jnp.abs(den), jnp.exp(-m_t))

            b_last = b_col[0:1, :] if rev else b_col[c - 1:c, :]
            c_new, n_new, m_new = _state_update(k, v, ig_col, b_col, b_last, c_old, n_old, m_old)
            c_out.append(c_new)
            n_out.append(n_new)
            m_out.append(jnp.broadcast_to(m_new, (1, LANES)))
    for u in range(2 * NH_A):
        c_scr[u] = c_out[u]
    n_scr[...] = jnp.concatenate(n_out, axis=0)
    m_scr[...] = jnp.concatenate(m_out, axis=0)


def _mlstm(qk, proj3, gates_col, gates_row, k_meta, proj_meta, gates_meta, bias_col, bias_row):
    bsz, t_len, _ = qk.shape
    c = MLSTM_CHUNK
    nc = t_len // c
    va_blk = OFF_VA // D_A
    fwd = lambda b, j: (b, j, 0)
    bwd = lambda b, j: (b, nc - 1 - j, 0)
    return pl.pallas_call(
        _mlstm_kernel,
        grid=(bsz, nc),
        in_specs=[
            pl.BlockSpec((1, c, 2 * QK_A), fwd),
            pl.BlockSpec((1, c, 2 * QK_A), bwd),
            pl.BlockSpec((1, c, D_A), lambda b, j: (b, j, va_blk)),
            pl.BlockSpec((1, c, D_A), lambda b, j: (b, nc - 1 - j, va_blk)),
            pl.BlockSpec((1, c, LANES), fwd),
            pl.BlockSpec((1, c, LANES), bwd),
            pl.BlockSpec((1, 4 * NH_A, c), lambda b, j: (b, 0, j)),
            pl.BlockSpec((1, 4 * NH_A, c), lambda b, j: (b, 0, nc - 1 - j)),
            pl.BlockSpec((1, N_META, QK_A), lambda b, j: (b, 0, 0)),
            pl.BlockSpec((N_META, D_A), lambda b, j: (0, va_blk)),
            pl.BlockSpec((N_META, LANES), lambda b, j: (0, 0)),
            pl.BlockSpec((1, LANES), lambda b, j: (0, 0)),
            pl.BlockSpec((4 * NH_A, 1), lambda b, j: (0, 0)),
        ],
        out_specs=[
            pl.BlockSpec((1, c, D_A), fwd),
            pl.BlockSpec((1, c, D_A), bwd),
        ],
        out_shape=[
            jax.ShapeDtypeStruct((bsz, t_len, D_A), F32),
            jax.ShapeDtypeStruct((bsz, t_len, D_A), F32),
        ],
        scratch_shapes=[
            pltpu.VMEM((2 * NH_A, DK_A, DV_A), F32),
            pltpu.VMEM((2 * NH_A, DK_A), F32),
            pltpu.VMEM((2 * NH_A, LANES), F32),
        ],
        compiler_params=_cparams(("arbitrary", "arbitrary")),
        name="mlstm",
    )(qk, qk, proj3, proj3, gates_col, gates_col, gates_row, gates_row,
      k_meta, proj_meta, gates_meta, bias_col, bias_row)


def _natten_bias_table(rpb):
    cols = np.arange(GRID_W)
    cstart = np.clip(cols - KC // 2, 0, GRID_W - KC)
    kc = cols[None, :]
    col_valid = (kc >= cstart[:, None]) & (kc < cstart[:, None] + KC)
    col_idx = np.clip(kc - cols[:, None] + KC - 1, 0, 2 * KC - 2)
    qr = np.arange(NAT_ROWS)[:, None]
    kk = np.arange(NAT_KROWS)[None, :]
    half = KR_MAX // 2
    rel_start = np.stack([np.maximum(qr - half, 0) + 0 * kk,
                          qr + 0 * kk,
                          np.minimum(qr + half, NAT_KROWS - KR_MAX) + 0 * kk])
    row_delta = np.stack([kk - qr, kk - qr - half, kk - qr - (NAT_KROWS - NAT_ROWS)])
    row_valid = (kk[None] >= rel_start) & (kk[None] < rel_start + KR_MAX)
    row_idx = np.clip(row_delta + KR_MAX - 1, 0, 2 * KR_MAX - 2)
    rpb_c = jnp.where(col_valid[None, None], rpb[:, :, col_idx], NEG_INF)
    tab = rpb_c[:, row_idx]
    tab = jnp.where(row_valid[None, :, :, :, None, None], tab, NEG_INF)
    tab = tab.transpose(1, 0, 2, 4, 3, 5)
    return tab.reshape(3, NH_B, NAT_ROWS * GRID_W, NAT_KROWS * GRID_W)


def _natten_kernel(q_ref, k_ref, v_ref, km_ref, vm_ref, bias_ref, o_ref):
    i = pl.program_id(2)
    rows = k_ref.shape[1] // GRID_W
    ks = jnp.clip(i * NAT_ROWS - KR_MAX // 2, 0, rows - NAT_KROWS)
    start = pl.multiple_of(ks * GRID_W, GRID_W * 4)
    nk = NAT_KROWS * GRID_W
    lane = lax.broadcasted_iota(jnp.int32, (1, LANES), 1)
    q = (q_ref[0] * (DH_B ** -0.5)).astype(BF16)
    kwin = k_ref[0, pl.ds(start, nk), :]
    vwin = v_ref[0, pl.ds(start, nk), :].astype(BF16)
    kmeta = km_ref[...]
    vmeta = vm_ref[...].astype(BF16)
    nt = (((1,), (1,)), ((), ()))
    out = jnp.zeros(q_ref.shape[1:], F32)
    for h in range(2):
        sel = (lane >= h * DH_B) & (lane < (h + 1) * DH_B)
        kh = jnp.where(sel, kwin, 0.0).astype(BF16)
        kmh = jnp.where(sel, kmeta, 0.0).astype(BF16)
        s = lax.dot_general(q, kh, nt, preferred_element_type=F32) + bias_ref[0, h]
        sm = lax.dot_general(q, kmh, nt, preferred_element_type=F32)
        mx = jnp.maximum(jnp.max(s, -1, keepdims=True), jnp.max(sm, -1, keepdims=True))
        p = jnp.exp(s - mx)
        pm = jnp.exp(sm - mx)
        denom = jnp.sum(p, -1, keepdims=True) + jnp.sum(pm, -1, keepdims=True)
        o = (jnp.dot(p.astype(BF16), vwin, preferred_element_type=F32)
             + jnp.dot(pm.astype(BF16), vmeta, preferred_element_type=F32)) / denom
        out = jnp.where(sel, o, out)
    o_ref[0] = out


def _natten(proj3, proj_meta, bias_tab):
    bsz, t_len, _ = proj3.shape
    rows = t_len // GRID_W
    assert rows % NAT_ROWS == 0 and rows >= NAT_KROWS
    nb = rows // NAT_ROWS
    nq = NAT_ROWS * GRID_W
    hp = NH_B // 2
    qb, kb, vb = OFF_QB // LANES, OFF_KB // LANES, OFF_VB // LANES

    def pat(i):
        return jnp.where(i == 0, 0, jnp.where(i == nb - 1, 2, 1))

    return pl.pallas_call(
        _natten_kernel,
        grid=(bsz, hp, nb),
        in_specs=[
            pl.BlockSpec((1, nq, LANES), lambda b, p, i: (b, i, qb + p)),
            pl.BlockSpec((1, t_len, LANES), lambda b, p, i: (b, 0, kb + p)),
            pl.BlockSpec((1, t_len, LANES), lambda b, p, i: (b, 0, vb + p)),
            pl.BlockSpec((N_META, LANES), lambda b, p, i: (0, kb + p)),
            pl.BlockSpec((N_META, LANES), lambda b, p, i: (0, vb + p)),
            pl.BlockSpec((1, 2, nq, NAT_KROWS * GRID_W), lambda b, p, i: (pat(i), p, 0, 0)),
        ],
        out_specs=pl.BlockSpec((1, nq, LANES), lambda b, p, i: (b, i, p)),
        out_shape=jax.ShapeDtypeStruct((bsz, t_len, D_B), F32),
        compiler_params=_cparams(("arbitrary", "arbitrary", "arbitrary")),
        name="natten",
    )(proj3, proj3, proj3, proj_meta, proj_meta, bias_tab)


def _merge_kernel(hf_ref, hb_ref, oa_ref, ga_ref, gb_ref, yb_ref, hg_ref, wa_ref, wb_ref, o_ref, ya_scr):
    for hd in range(NH_A):
        sl = slice(hd * DV_A, (hd + 1) * DV_A)
        h = hf_ref[:, sl] + hb_ref[:, sl]
        hc = h - jnp.mean(h, -1, keepdims=True)
        var = jnp.mean(hc * hc, -1, keepdims=True)
        hn = hc * lax.rsqrt(var + LN_EPS) * hg_ref[:, sl]
        ya_scr[:, sl] = (hn * jax.nn.sigmoid(oa_ref[:, sl])).astype(BF16)
    a = jnp.dot(ya_scr[...], wa_ref[...], preferred_element_type=F32)
    b = jnp.dot(yb_ref[...].astype(BF16), wb_ref[...], preferred_element_type=F32)
    o_ref[...] = (jax.nn.sigmoid(ga_ref[...]) * a + jax.nn.sigmoid(gb_ref[...]) * b).astype(BF16)


def _merge(h_f, h_b, proj, y_b, head_g, w_a, w_b):
    n = h_f.shape[0]
    tm = MERGE_TM
    const = lambda i: (0, 0)
    return pl.pallas_call(
        _merge_kernel,
        grid=(n // tm,),
        in_specs=[
            pl.BlockSpec((tm, D_A), lambda i: (i, 0)),
            pl.BlockSpec((tm, D_A), lambda i: (i, 0)),
            pl.BlockSpec((tm, D_A), lambda i: (i, OFF_OA // D_A)),
            pl.BlockSpec((tm, D_MODEL), lambda i: (i, OFF_GA // D_MODEL)),
            pl.BlockSpec((tm, D_MODEL), lambda i: (i, OFF_GB // D_MODEL)),
            pl.BlockSpec((tm, D_B), lambda i: (i, 0)),
            pl.BlockSpec((1, D_A), const),
            pl.BlockSpec((D_A, D_MODEL), const, pipeline_mode=pl.Buffered(1)),
            pl.BlockSpec((D_B, D_MODEL), const, pipeline_mode=pl.Buffered(1)),
        ],
        out_specs=pl.BlockSpec((tm, D_MODEL), lambda i: (i, 0)),
        out_shape=jax.ShapeDtypeStruct((n, D_MODEL), BF16),
        scratch_shapes=[pltpu.VMEM((tm, D_A), BF16)],
        compiler_params=_cparams(("arbitrary",)),
        name="merge",
    )(h_f, h_b, proj, proj, proj, y_b, head_g, w_a, w_b)


def _outproj_router_kernel(m_ref, x_ref, lg_ref, lb_ref, wo_ref, g1_ref, b1_ref, wr_hi_ref, wr_lo_ref, br_ref,
                           h1_ref, route_ref):
    mix = jnp.dot(m_ref[...], wo_ref[...], preferred_element_type=F32)
    h0 = _layer_norm(x_ref[...], lg_ref[...], lb_ref[...])
    h1 = _layer_norm(ALPHA * h0 + mix, g1_ref[...], b1_ref[...])
    h1_ref[...] = h1

    hi = h1.astype(BF16)
    lo = (h1 - hi.astype(F32)).astype(BF16)
    d = functools.partial(jnp.dot, preferred_element_type=F32)
    logits = d(hi, wr_hi_ref[...]) + (d(lo, wr_hi_ref[...]) + d(hi, wr_lo_ref[...])) + br_ref[...]

    lane = lax.broadcasted_iota(jnp.int32, logits.shape, 1)
    big = jnp.int32(LANES)
    is_g = lane < N_GROUPS
    gl = jnp.where(is_g, logits, NEG_INF)
    gmax = jnp.max(gl, -1, keepdims=True)
    gsel = jnp.min(jnp.where(is_g & (gl == gmax), lane, big), -1, keepdims=True)
    pg = 1.0 / jnp.sum(jnp.where(is_g, jnp.exp(gl - gmax), 0.0), -1, keepdims=True)
    e_lo = N_GROUPS + gsel * EXPERTS_PER_GROUP
    in_grp = (lane >= e_lo) & (lane < e_lo + EXPERTS_PER_GROUP)
    el = jnp.where(in_grp, logits, NEG_INF)
    v0 = jnp.max(el, -1, keepdims=True)
    i0 = jnp.min(jnp.where(in_grp & (el == v0), lane, big), -1, keepdims=True)
    el1 = jnp.where(lane == i0, NEG_INF, el)
    v1 = jnp.max(el1, -1, keepdims=True)
    i1 = jnp.min(jnp.where(in_grp & (lane != i0) & (el1 == v1), lane, big), -1, keepdims=True)
    e1 = jnp.exp(v1 - v0)
    den = 1.0 + e1
    g0 = pg * (1.0 / den)
    g1 = pg * (e1 / den)
    out = jnp.where(lane == 0, (i0 - N_GROUPS).astype(F32),
                    jnp.where(lane == 1, (i1 - N_GROUPS).astype(F32),
                              jnp.where(lane == 2, g0, jnp.where(lane == 3, g1, 0.0))))
    route_ref[...] = out


def _outproj_router(merged, x, ln_in_g, ln_in_b, w_out, ln1_g, ln1_b, wr_hi, wr_lo, br):
    n = x.shape[0]
    tm = OUT_TM
    const = lambda i: (0, 0)
    row = lambda i: (i, 0)
    return pl.pallas_call(
        _outproj_router_kernel,
        grid=(n // tm,),
        in_specs=[
            pl.BlockSpec((tm, D_MODEL), row),
            pl.BlockSpec((tm, D_MODEL), row),
            pl.BlockSpec((1, D_MODEL), const),
            pl.BlockSpec((1, D_MODEL), const),
            pl.BlockSpec((D_MODEL, D_MODEL), const, pipeline_mode=pl.Buffered(1)),
            pl.BlockSpec((1, D_MODEL), const),
            pl.BlockSpec((1, D_MODEL), const),
            pl.BlockSpec((D_MODEL, LANES), const),
            pl.BlockSpec((D_MODEL, LANES), const),
            pl.BlockSpec((1, LANES), const),
        ],
        out_specs=[pl.BlockSpec((tm, D_MODEL), row), pl.BlockSpec((tm, LANES), row)],
        out_shape=[jax.ShapeDtypeStruct((n, D_MODEL), F32), jax.ShapeDtypeStruct((n, LANES), F32)],
        compiler_params=_cparams(("arbitrary",)),
        name="outproj_router",
    )(merged, x, ln_in_g, ln_in_b, w_out, ln1_g, ln1_b, wr_hi, wr_lo, br)


def _moe_ffn_kernel(be_ref, nu_ref, tok_ref, tokn_ref, h1_hbm, wg_ref, wu_ref, wd_ref, y_ref, xbuf, sem):
    i = pl.program_id(0)
    n_used = nu_ref[0]
    bm = xbuf.shape[1]
    slot = lax.rem(i, 2)
    nxt = 1 - slot

    def start_rows(t_ref, dst_slot, r0, r1):
        for r in range(r0, r1):
            pltpu.make_async_copy(h1_hbm.at[pl.ds(t_ref[0, 0, r], 1)], xbuf.at[dst_slot, pl.ds(r, 1)],
                                  sem.at[dst_slot]).start()

    def wait_rows(s):
        pltpu.make_async_copy(h1_hbm.at[pl.ds(0, bm)], xbuf.at[s], sem.at[s]).wait()

    @pl.when(i == 0)
    def _():
        start_rows(tok_ref, 0, 0, bm)

    @pl.when(i < n_used)
    def _():
        wait_rows(slot)
        xb = xbuf[slot].astype(BF16)
        n_piece = 2 * MOE_NCHUNK
        rows = bm // n_piece
        ce = D_EXPERT // MOE_NCHUNK
        hs = []
        for c in range(MOE_NCHUNK):
            start_rows(tokn_ref, nxt, c * rows, (c + 1) * rows)
            g = jnp.dot(xb, wg_ref[0, :, c * ce:(c + 1) * ce], preferred_element_type=F32)
            u = jnp.dot(xb, wu_ref[0, :, c * ce:(c + 1) * ce], preferred_element_type=F32)
            hs.append((jax.nn.silu(g) * u).astype(BF16))
        hb = jnp.concatenate(hs, axis=1)
        cd = D_MODEL // MOE_NCHUNK
        for c in range(MOE_NCHUNK):
            p = MOE_NCHUNK + c
            start_rows(tokn_ref, nxt, p * rows, (p + 1) * rows)
            y_ref[:, c * cd:(c + 1) * cd] = jnp.dot(hb, wd_ref[0, :, c * cd:(c + 1) * cd],
                                                     preferred_element_type=F32)

    @pl.when(i >= n_used)
    def _():
        @pl.when(i == n_used)
        def _():
            wait_rows(slot)

        y_ref[...] = jnp.zeros(y_ref.shape, F32)


def _moe_ffn(block_e, n_used, tok3, h1, w_gate, w_up, w_down):
    nb = tok3.shape[0]
    bm = MOE_BM
    wmap = lambda i, be, nu: (be[i], 0, 0)
    grid_spec = pltpu.PrefetchScalarGridSpec(
        num_scalar_prefetch=2,
        grid=(nb,),
        in_specs=[
            pl.BlockSpec((1, 1, bm), lambda i, be, nu: (i, 0, 0), memory_space=pltpu.SMEM),
            pl.BlockSpec((1, 1, bm), lambda i, be, nu: (jnp.minimum(i + 1, nb - 1), 0, 0),
                         memory_space=pltpu.SMEM),
            pl.BlockSpec(memory_space=pl.ANY),
            pl.BlockSpec((1, D_MODEL, D_EXPERT), wmap),
            pl.BlockSpec((1, D_MODEL, D_EXPERT), wmap),
            pl.BlockSpec((1, D_EXPERT, D_MODEL), wmap),
        ],
        out_specs=pl.BlockSpec((bm, D_MODEL), lambda i, be, nu: (i, 0)),
        scratch_shapes=[pltpu.VMEM((2, bm, D_MODEL), F32), pltpu.SemaphoreType.DMA((2,))],
    )
    return pl.pallas_call(
        _moe_ffn_kernel,
        grid_spec=grid_spec,
        out_shape=jax.ShapeDtypeStruct((nb * bm, D_MODEL), F32),
        compiler_params=_cparams(("arbitrary",)),
        name="moe_ffn",
    )(block_e, n_used, tok3, tok3, h1, w_gate, w_up, w_down)


def _combine_kernel(pos_ref, y_hbm, h1_ref, route_ref, g_ref, b_ref, o_ref, buf0, buf1, sem):
    tm = buf0.shape[0]

    def issue(r, carry):
        p0 = pos_ref[0, 0, 2 * r]
        p1 = pos_ref[0, 0, 2 * r + 1]
        pltpu.make_async_copy(y_hbm.at[pl.ds(p0, 1)], buf0.at[pl.ds(r, 1)], sem.at[0]).start()
        pltpu.make_async_copy(y_hbm.at[pl.ds(p1, 1)], buf1.at[pl.ds(r, 1)], sem.at[1]).start()
        return carry

    lax.fori_loop(0, tm, issue, 0, unroll=8)
    pltpu.make_async_copy(y_hbm.at[pl.ds(0, tm)], buf0, sem.at[0]).wait()
    pltpu.make_async_copy(y_hbm.at[pl.ds(0, tm)], buf1, sem.at[1]).wait()
    route = route_ref[...]
    ffn = buf0[...] * route[:, 2:3] + buf1[...] * route[:, 3:4]
    o_ref[...] = _layer_norm(ALPHA * h1_ref[...] + ffn, g_ref[...], b_ref[...])


def _combine(pos3, y_sorted, h1, route, ln2_g, ln2_b):
    n = h1.shape[0]
    tm = COMB_TM
    const = lambda i: (0, 0)
    row = lambda i: (i, 0)
    return pl.pallas_call(
        _combine_kernel,
        grid=(n // tm,),
        in_specs=[
            pl.BlockSpec((1, 1, 2 * tm), lambda i: (i, 0, 0), memory_space=pltpu.SMEM),
            pl.BlockSpec(memory_space=pl.ANY),
            pl.BlockSpec((tm, D_MODEL), row),
            pl.BlockSpec((tm, LANES), row),
            pl.BlockSpec((1, D_MODEL), const),
            pl.BlockSpec((1, D_MODEL), const),
        ],
        out_specs=pl.BlockSpec((tm, D_MODEL), row),
        out_shape=jax.ShapeDtypeStruct((n, D_MODEL), F32),
        scratch_shapes=[pltpu.VMEM((tm, D_MODEL), F32), pltpu.VMEM((tm, D_MODEL), F32),
                        pltpu.SemaphoreType.DMA((2,))],
        compiler_params=_cparams(("arbitrary",)),
        name="combine",
    )(pos3, y_sorted, h1, route, ln2_g, ln2_b)


def _dispatch_plan(route, n):
    bm = MOE_BM
    a = 2 * n
    nb = a // bm + N_EXPERTS
    flat_e = route[:, 0:2].astype(jnp.int32).reshape(a)
    onehot = (flat_e[:, None] == jnp.arange(N_EXPERTS, dtype=jnp.int32)[None, :]).astype(jnp.int32)
    csum = jnp.cumsum(onehot, axis=0)
    rank = jnp.take_along_axis(csum, flat_e[:, None], axis=1)[:, 0] - 1
    counts = csum[-1]
    pcounts = (counts + bm - 1) // bm * bm
    pend = jnp.cumsum(pcounts)
    pstart = pend - pcounts
    pos = pstart[flat_e] + rank
    tok = jnp.zeros((nb * bm,), jnp.int32).at[pos].set(jnp.arange(a, dtype=jnp.int32) // 2)
    block_e = jnp.minimum(
        jnp.searchsorted(pend, jnp.arange(nb, dtype=jnp.int32) * bm, side="right"), N_EXPERTS - 1
    ).astype(jnp.int32)
    n_used = (pend[-1:] // bm).astype(jnp.int32)
    return block_e, n_used, tok.reshape(nb, 1, bm), pos.reshape(n // COMB_TM, 1, 2 * COMB_TM)


def _encode(x, p):
    bsz, t_len, _ = x.shape
    n = bsz * t_len
    x2 = x.reshape(n, D_MODEL)
    proj, gates = _inproj(x2, p["ln_in_g"], p["ln_in_b"], p["w_main"], p["w_gates"])
    proj3 = proj.reshape(bsz, t_len, P_MAIN)
    gates3 = gates.reshape(bsz, t_len, LANES)
    gates_row = gates3[:, :, :4 * NH_A].transpose(0, 2, 1)
    qk, k_meta = _conv(proj3, p["proj_meta"], p["conv_w"], p["conv_b"])
    h_f, h_b = _mlstm(qk, proj3, gates3, gates_row, k_meta, p["proj_meta"], p["gates_meta"],
                      p["bias_col"], p["bias_row"])
    y_b = _natten(proj3, p["proj_meta"], p["bias_tab"])
    merged = _merge(h_f.reshape(n, D_A), h_b.reshape(n, D_A), proj, y_b.reshape(n, D_B),
                    p["head_g"], p["w_a"], p["w_b"])
    h1, route = _outproj_router(merged, x2, p["ln_in_g"], p["ln_in_b"], p["w_out"], p["ln1_g"], p["ln1_b"],
                                p["wr_hi"], p["wr_lo"], p["br"])
    block_e, n_used, tok3, pos3 = _dispatch_plan(route, n)
    y_sorted = _moe_ffn(block_e, n_used, tok3, h1, p["w_gate"], p["w_up"], p["w_down"])
    out = _combine(pos3, y_sorted, h1, route, p["ln2_g"], p["ln2_b"])
    return out.reshape(bsz, t_len, D_MODEL)


def kernel(x_prompt, x_sample, meta_tokens, ln_in_g, ln_in_b, w_in, b_gates, conv_w, conv_b, head_g, rpb, w_a, w_b, w_out, ln1_g, ln1_b, router_g_w, router_g_b, router_e_w, router_e_b, w_gate, w_up, w_down, ln2_g, ln2_b):
    l = 0
    offs = np.cumsum((0,) + IN_SPLITS)
    col = lambda i: w_in[l][:, offs[i]:offs[i + 1]]
    w_main = jnp.concatenate([col(0), col(1), col(2), col(3), col(8), col(9), col(5), col(6), col(7)],
                             axis=1).astype(BF16)
    w_gates = jnp.pad(col(4), ((0, 0), (0, LANES - 4 * NH_A))).astype(BF16)
    row = lambda v: v.reshape(1, -1).astype(F32)
    wr = jnp.pad(jnp.concatenate([router_g_w[l], router_e_w[l]], axis=1),
                 ((0, 0), (0, LANES - N_GROUPS - N_EXPERTS)))
    wr_hi = wr.astype(BF16)
    p = {
        "ln_in_g": row(ln_in_g), "ln_in_b": row(ln_in_b),
        "w_main": w_main, "w_gates": w_gates,
        "conv_w": conv_w[l], "conv_b": row(conv_b[l]),
        "bias_col": jnp.pad(b_gates[l], (0, LANES - 4 * NH_A)).reshape(1, LANES),
        "bias_row": b_gates[l].reshape(4 * NH_A, 1),
        "head_g": row(head_g[l]),
        "bias_tab": _natten_bias_table(rpb[l]),
        "w_a": w_a[l].astype(BF16), "w_b": w_b[l].astype(BF16), "w_out": w_out[l].astype(BF16),
        "ln1_g": row(ln1_g[l]), "ln1_b": row(ln1_b[l]),
        "wr_hi": wr_hi, "wr_lo": (wr - wr_hi.astype(F32)).astype(BF16),
        "br": jnp.pad(jnp.concatenate([router_g_b[l], router_e_b[l]]),
                      (0, LANES - N_GROUPS - N_EXPERTS)).reshape(1, LANES),
        "w_gate": w_gate[l].astype(BF16), "w_up": w_up[l].astype(BF16), "w_down": w_down[l].astype(BF16),
        "ln2_g": row(ln2_g[l]), "ln2_b": row(ln2_b[l]),
    }
    p["proj_meta"], p["gates_meta"] = _inproj(meta_tokens, p["ln_in_g"], p["ln_in_b"], w_main, w_gates)
    return (_encode(x_prompt, p), _encode(x_sample, p))
```

```python
import functools

import jax
import jax.numpy as jnp
import numpy as np
from jax import lax
from jax.experimental import pallas as pl
from jax.experimental.pallas import tpu as pltpu

F32 = jnp.float32
BF16 = jnp.bfloat16

D_MODEL = 2048
N_META = 16
GRID_W = 64
NH_A = 8
DK_A = 128
DV_A = 256
QK_A = NH_A * DK_A
D_A = NH_A * DV_A
NH_B = 16
DH_B = 64
D_B = NH_B * DH_B
KR_MAX = 8
KC = 16
N_GROUPS = 4
EXPERTS_PER_GROUP = 8
N_EXPERTS = N_GROUPS * EXPERTS_PER_GROUP
D_EXPERT = 1024
DEPTH = 1
ALPHA = (2 * DEPTH) ** 0.25
LN_EPS = 1e-5
NEG_INF = -1e30
IN_SPLITS = (QK_A, QK_A, D_A, D_A, 4 * NH_A, D_B, D_B, D_B, D_MODEL, D_MODEL)

OFF_QK = 0
OFF_VA = 2048
OFF_OA = 4096
OFF_GA = 6144
OFF_GB = 8192
OFF_QB = 10240
OFF_KB = 11264
OFF_VB = 12288
P_MAIN = 13312
LANES = 128

VMEM_LIMIT = 56 * 1024 * 1024

INPROJ_TM = 1024
INPROJ_TN = 1024
INPROJ_LN_SLAB = 256
CONV_TC = 512
MLSTM_CHUNK = 256
NAT_ROWS = 8
NAT_KROWS = 16
MERGE_TM = 256
OUT_TM = 512
MOE_BM = 256
COMB_TM = 256


def _cparams(sem):
    return pltpu.CompilerParams(dimension_semantics=sem, vmem_limit_bytes=VMEM_LIMIT)


def _layer_norm(x, g, b):
    xc = x - jnp.mean(x, -1, keepdims=True)
    var = jnp.mean(xc * xc, -1, keepdims=True)
    return xc * lax.rsqrt(var + LN_EPS) * g + b


def _inproj_kernel(x_ref, g_ref, b_ref, w_ref, wg_ref, o_ref, og_ref, h_scr):
    @pl.when(pl.program_id(1) == 0)
    def _():
        tm = x_ref.shape[0]
        slab = min(INPROJ_LN_SLAB, tm)

        def ln_slab(s, carry):
            r = pl.multiple_of(s * slab, slab)
            h = _layer_norm(x_ref[pl.ds(r, slab), :], g_ref[...], b_ref[...]).astype(BF16)
            h_scr[pl.ds(r, slab), :] = h
            og_ref[pl.ds(r, slab), :] = jnp.dot(h, wg_ref[...], preferred_element_type=F32)
            return carry

        lax.fori_loop(0, tm // slab, ln_slab, 0)

    o_ref[...] = jnp.dot(h_scr[...], w_ref[...], preferred_element_type=F32)


def _inproj(x, ln_g, ln_b, w_main, w_gates):
    n = x.shape[0]
    tm = min(INPROJ_TM, n)
    tn = INPROJ_TN
    return pl.pallas_call(
        _inproj_kernel,
        grid=(n // tm, P_MAIN // tn),
        in_specs=[
            pl.BlockSpec((tm, D_MODEL), lambda i, j: (i, 0)),
            pl.BlockSpec((1, D_MODEL), lambda i, j: (0, 0)),
            pl.BlockSpec((1, D_MODEL), lambda i, j: (0, 0)),
            pl.BlockSpec((D_MODEL, tn), lambda i, j: (0, j)),
            pl.BlockSpec((D_MODEL, LANES), lambda i, j: (0, 0)),
        ],
        out_specs=[
            pl.BlockSpec((tm, tn), lambda i, j: (i, j)),
            pl.BlockSpec((tm, LANES), lambda i, j: (i, 0)),
        ],
        out_shape=[
            jax.ShapeDtypeStruct((n, P_MAIN), F32),
            jax.ShapeDtypeStruct((n, LANES), F32),
        ],
        scratch_shapes=[pltpu.VMEM((tm, D_MODEL), BF16)],
        compiler_params=_cparams(("arbitrary", "arbitrary")),
        name="inproj",
    )(x, ln_g, ln_b, w_main, w_gates)


def _conv_kernel(x_ref, prev_ref, next_ref, meta_ref, w_ref, b_ref, qk_ref, kmeta_ref):
    t = pl.program_id(1)
    last = pl.num_programs(1) - 1
    tc = x_ref.shape[1]
    w0, w1, w2 = w_ref[0:1, :], w_ref[1:2, :], w_ref[2:3, :]
    bias = b_ref[...]
    lane = lax.broadcasted_iota(jnp.int32, (1, 2 * QK_A), 1)
    out_scale = jnp.where(lane >= QK_A, DK_A ** -0.5, 1.0).astype(F32)

    x = x_ref[0]
    meta = meta_ref[...]
    prev_row = jnp.where(t == 0, meta[N_META - 1:N_META, :], prev_ref[0, 7:8, :])
    next_row = jnp.where(t == last, jnp.zeros_like(prev_row), next_ref[0, 0:1, :])
    row = lax.broadcasted_iota(jnp.int32, (tc, 1), 0)
    x_prev = jnp.where(row == 0, prev_row, pltpu.roll(x, 1, 0))
    x_next = jnp.where(row == tc - 1, next_row, pltpu.roll(x, tc - 1, 0))
    y = x_prev * w0 + x * w1 + x_next * w2 + bias
    qk_ref[0] = (jax.nn.silu(y) * out_scale).astype(BF16)

    @pl.when(t == 0)
    def _():
        mrow = lax.broadcasted_iota(jnp.int32, (N_META, 1), 0)
        m_prev = jnp.where(mrow == 0, 0.0, pltpu.roll(meta, 1, 0))
        m_next = jnp.where(mrow == N_META - 1, x[0:1, :], pltpu.roll(meta, N_META - 1, 0))
        ym = m_prev * w0 + meta * w1 + m_next * w2 + bias
        km = jax.nn.silu(ym) * out_scale
        kmeta_ref[0] = km[:, QK_A:].astype(BF16)


def _conv(proj3, proj_meta, conv_w, conv_b):
    bsz, t_len, _ = proj3.shape
    tc = CONV_TC
    nt = t_len // tc
    r8 = tc // 8
    return pl.pallas_call(
        _conv_kernel,
        grid=(bsz, nt),
        in_specs=[
            pl.BlockSpec((1, tc, 2 * QK_A), lambda b, t: (b, t, 0)),
            pl.BlockSpec((1, 8, 2 * QK_A), lambda b, t: (b, jnp.maximum(t * r8 - 1, 0), 0)),
            pl.BlockSpec((1, 8, 2 * QK_A), lambda b, t: (b, jnp.minimum((t + 1) * r8, t_len // 8 - 1), 0)),
            pl.BlockSpec((N_META, 2 * QK_A), lambda b, t: (0, 0)),
            pl.BlockSpec((3, 2 * QK_A), lambda b, t: (0, 0)),
            pl.BlockSpec((1, 2 * QK_A), lambda b, t: (0, 0)),
        ],
        out_specs=[
            pl.BlockSpec((1, tc, 2 * QK_A), lambda b, t: (b, t, 0)),
            pl.BlockSpec((1, N_META, QK_A), lambda b, t: (b, 0, 0)),
        ],
        out_shape=[
            jax.ShapeDtypeStruct((bsz, t_len, 2 * QK_A), BF16),
            jax.ShapeDtypeStruct((bsz, N_META, QK_A), BF16),
        ],
        compiler_params=_cparams(("arbitrary", "arbitrary")),
        name="conv_silu",
    )(proj3, proj3, proj3, proj_meta, conv_w, conv_b)


def _split3(x):
    hi = x.astype(BF16)
    r1 = x - hi.astype(F32)
    mid = r1.astype(BF16)
    lo = (r1 - mid.astype(F32)).astype(BF16)
    return hi, mid, lo


def _tri_left(tri, x):
    hi, mid, lo = _split3(x)
    d = functools.partial(jnp.dot, preferred_element_type=F32)
    return d(tri, hi) + d(tri, mid) + d(tri, lo)


def _tri_right(x, tri):
    hi, mid, lo = _split3(x)
    d = functools.partial(jnp.dot, preferred_element_type=F32)
    return d(hi, tri) + d(mid, tri) + d(lo, tri)


def _state_update(k, v, ig_col, b_col, b_last, c_old, n_old, m_old):
    logw = b_last - b_col + ig_col
    m_new = jnp.maximum(b_last + m_old, jnp.max(logw, axis=0, keepdims=True))
    w = jnp.exp(logw - m_new)
    decay = jnp.exp(b_last + m_old - m_new)
    kw = k.astype(F32) * w
    kv = lax.dot_general(kw.astype(BF16), v, (((0,), (0,)), ((), ())), preferred_element_type=F32)
    c_new = decay * c_old + kv
    n_new = decay * n_old + jnp.sum(kw, axis=0, keepdims=True)
    return c_new, n_new, m_new


def _mlstm_kernel(qkf_ref, qkb_ref, vf_ref, vb_ref, gcf_ref, gcb_ref, grf_ref, grb_ref,
                  kmeta_ref, vmeta_ref, gmeta_ref, bcol_ref, brow_ref,
                  hf_ref, hb_ref, c_scr, n_scr, m_scr):
    j = pl.program_id(1)
    c = qkf_ref.shape[1]
    r_i = lax.broadcasted_iota(jnp.int32, (c, c), 0)
    c_i = lax.broadcasted_iota(jnp.int32, (c, c), 1)
    lower = r_i >= c_i
    upper = r_i <= c_i
    tri_l = lower.astype(BF16)
    tri_u = upper.astype(BF16)

    @pl.when(j == 0)
    def _():
        gm = gmeta_ref[...] + bcol_ref[...]
        lfm = jax.nn.log_sigmoid(gm)
        mr = lax.broadcasted_iota(jnp.int32, (N_META, N_META), 0)
        mc = lax.broadcasted_iota(jnp.int32, (N_META, N_META), 1)
        bm = _tri_left((mr >= mc).astype(BF16), lfm)
        zero_m = jnp.zeros((1, 1), F32)
        for hd in range(NH_A):
            ig_col = gm[:, hd:hd + 1]
            b_col = bm[:, 2 * NH_A + hd:2 * NH_A + hd + 1]
            b_last = b_col[N_META - 1:N_META, :]
            k = kmeta_ref[0, :, hd * DK_A:(hd + 1) * DK_A]
            v = vmeta_ref[:, hd * DV_A:(hd + 1) * DV_A].astype(BF16)
            c_new, n_new, m_new = _state_update(
                k, v, ig_col, b_col, b_last,
                jnp.zeros((DK_A, DV_A), F32), jnp.zeros((1, DK_A), F32), zero_m)
            c_scr[hd] = c_new
            n_scr[hd:hd + 1, :] = n_new
            m_scr[hd:hd + 1, :] = jnp.broadcast_to(m_new, (1, LANES))
        for hd in range(NH_A):
            u = NH_A + hd
            c_scr[u] = jnp.zeros((DK_A, DV_A), F32)
            n_scr[u:u + 1, :] = jnp.zeros((1, DK_A), F32)
            m_scr[u:u + 1, :] = jnp.zeros((1, LANES), F32)

    c_all = [c_scr[u] for u in range(2 * NH_A)]
    n_all = n_scr[...]
    m_all = m_scr[...]
    c_out, n_out, m_out = [], [], []
    for direction in range(2):
        rev = direction == 1
        qk_ref = qkb_ref if rev else qkf_ref
        v_ref = vb_ref if rev else vf_ref
        gc = (gcb_ref if rev else gcf_ref)[0] + bcol_ref[...]
        gr = (grb_ref if rev else grf_ref)[0] + brow_ref[...]
        lf_c = jax.nn.log_sigmoid(gc)
        lf_r = jax.nn.log_sigmoid(gr)
        if rev:
            bc_all = _tri_left(tri_u, lf_c)
            br_all = _tri_right(lf_r, tri_l)
            mask = upper
        else:
            bc_all = _tri_left(tri_l, lf_c)
            br_all = _tri_right(lf_r, tri_u)
            mask = lower
        out_ref = hb_ref if rev else hf_ref
        for hd in range(NH_A):
            u = direction * NH_A + hd
            ig_lane = direction * NH_A + hd
            lf_lane = 2 * NH_A + direction * NH_A + hd
            ig_col = gc[:, ig_lane:ig_lane + 1]
            b_col = bc_all[:, lf_lane:lf_lane + 1]
            ig_row = gr[ig_lane:ig_lane + 1, :]
            b_row = br_all[lf_lane:lf_lane + 1, :]
            q = qk_ref[0, :, hd * DK_A:(hd + 1) * DK_A]
            k = qk_ref[0, :, QK_A + hd * DK_A:QK_A + (hd + 1) * DK_A]
            v = v_ref[0, :, hd * DV_A:(hd + 1) * DV_A].astype(BF16)
            c_old = c_all[u]
            n_old = n_all[u:u + 1, :]
            m_old = m_all[u:u + 1, 0:1]

            logd = jnp.where(mask, b_col + (ig_row - b_row), NEG_INF)
            log_inter = b_col + m_old
            m_t = jnp.maximum(log_inter, jnp.max(logd, axis=-1, keepdims=True))
            s = lax.dot_general(q, k, (((1,), (1,)), ((), ())), preferred_element_type=F32)
            s = s * jnp.exp(logd - m_t)
            w_inter = jnp.exp(log_inter - m_t)
            q_c = jnp.dot(q, c_old.astype(BF16), preferred_element_type=F32)
            q_n = jnp.sum(q.astype(F32) * n_old, axis=-1, keepdims=True)
            num = jnp.dot(s.astype(BF16), v, preferred_element_type=F32) + w_inter * q_c
            den = jnp.sum(s, axis=-1, keepdims=True) + w_inter * q_n
            out_ref[0, :, hd * DV_A:(hd + 1) * DV_A] = num / jnp.maximum(jnp.abs(den), jnp.exp(-m_t))

            b_last = b_col[0:1, :] if rev else b_col[c - 1:c, :]
            c_new, n_new, m_new = _state_update(k, v, ig_col, b_col, b_last, c_old, n_old, m_old)
            c_out.append(c_new)
            n_out.append(n_new)
            m_out.append(jnp.broadcast_to(m_new, (1, LANES)))
    for u in range(2 * NH_A):
        c_scr[u] = c_out[u]
    n_scr[...] = jnp.concatenate(n_out, axis=0)
    m_scr[...] = jnp.concatenate(m_out, axis=0)


def _mlstm(qk, proj3, gates_col, gates_row, k_meta, proj_meta, gates_meta, bias_col, bias_row):
    bsz, t_len, _ = qk.shape
    c = MLSTM_CHUNK
    nc = t_len // c
    va_blk = OFF_VA // D_A
    fwd = lambda b, j: (b, j, 0)
    bwd = lambda b, j: (b, nc - 1 - j, 0)
    return pl.pallas_call(
        _mlstm_kernel,
        grid=(bsz, nc),
        in_specs=[
            pl.BlockSpec((1, c, 2 * QK_A), fwd),
            pl.BlockSpec((1, c, 2 * QK_A), bwd),
            pl.BlockSpec((1, c, D_A), lambda b, j: (b, j, va_blk)),
            pl.BlockSpec((1, c, D_A), lambda b, j: (b, nc - 1 - j, va_blk)),
            pl.BlockSpec((1, c, LANES), fwd),
            pl.BlockSpec((1, c, LANES), bwd),
            pl.BlockSpec((1, 4 * NH_A, c), lambda b, j: (b, 0, j)),
            pl.BlockSpec((1, 4 * NH_A, c), lambda b, j: (b, 0, nc - 1 - j)),
            pl.BlockSpec((1, N_META, QK_A), lambda b, j: (b, 0, 0)),
            pl.BlockSpec((N_META, D_A), lambda b, j: (0, va_blk)),
            pl.BlockSpec((N_META, LANES), lambda b, j: (0, 0)),
            pl.BlockSpec((1, LANES), lambda b, j: (0, 0)),
            pl.BlockSpec((4 * NH_A, 1), lambda b, j: (0, 0)),
        ],
        out_specs=[
            pl.BlockSpec((1, c, D_A), fwd),
            pl.BlockSpec((1, c, D_A), bwd),
        ],
        out_shape=[
            jax.ShapeDtypeStruct((bsz, t_len, D_A), F32),
            jax.ShapeDtypeStruct((bsz, t_len, D_A), F32),
        ],
        scratch_shapes=[
            pltpu.VMEM((2 * NH_A, DK_A, DV_A), F32),
            pltpu.VMEM((2 * NH_A, DK_A), F32),
            pltpu.VMEM((2 * NH_A, LANES), F32),
        ],
        compiler_params=_cparams(("arbitrary", "arbitrary")),
        name="mlstm",
    )(qk, qk, proj3, proj3, gates_col, gates_col, gates_row, gates_row,
      k_meta, proj_meta, gates_meta, bias_col, bias_row)


def _natten_bias_table(rpb):
    cols = np.arange(GRID_W)
    cstart = np.clip(cols - KC // 2, 0, GRID_W - KC)
    kc = cols[None, :]
    col_valid = (kc >= cstart[:, None]) & (kc < cstart[:, None] + KC)
    col_idx = np.clip(kc - cols[:, None] + KC - 1, 0, 2 * KC - 2)
    qr = np.arange(NAT_ROWS)[:, None]
    kk = np.arange(NAT_KROWS)[None, :]
    half = KR_MAX // 2
    rel_start = np.stack([np.maximum(qr - half, 0) + 0 * kk,
                          qr + 0 * kk,
                          np.minimum(qr + half, NAT_KROWS - KR_MAX) + 0 * kk])
    row_delta = np.stack([kk - qr, kk - qr - half, kk - qr - (NAT_KROWS - NAT_ROWS)])
    row_valid = (kk[None] >= rel_start) & (kk[None] < rel_start + KR_MAX)
    row_idx = np.clip(row_delta + KR_MAX - 1, 0, 2 * KR_MAX - 2)
    rpb_c = jnp.where(col_valid[None, None], rpb[:, :, col_idx], NEG_INF)
    tab = rpb_c[:, row_idx]
    tab = jnp.where(row_valid[None, :, :, :, None, None], tab, NEG_INF)
    tab = tab.transpose(1, 0, 2, 4, 3, 5)
    return tab.reshape(3, NH_B, NAT_ROWS * GRID_W, NAT_KROWS * GRID_W)


def _natten_kernel(q_ref, k_ref, v_ref, km_ref, vm_ref, bias_ref, o_ref):
    i = pl.program_id(2)
    rows = k_ref.shape[1] // GRID_W
    ks = jnp.clip(i * NAT_ROWS - KR_MAX // 2, 0, rows - NAT_KROWS)
    start = pl.multiple_of(ks * GRID_W, GRID_W * 4)
    nk = NAT_KROWS * GRID_W
    lane = lax.broadcasted_iota(jnp.int32, (1, LANES), 1)
    q = (q_ref[0] * (DH_B ** -0.5)).astype(BF16)
    kwin = k_ref[0, pl.ds(start, nk), :]
    vwin = v_ref[0, pl.ds(start, nk), :].astype(BF16)
    kmeta = km_ref[...]
    vmeta = vm_ref[...].astype(BF16)
    nt = (((1,), (1,)), ((), ()))
    out = jnp.zeros(q_ref.shape[1:], F32)
    for h in range(2):
        sel = (lane >= h * DH_B) & (lane < (h + 1) * DH_B)
        kh = jnp.where(sel, kwin, 0.0).astype(BF16)
        kmh = jnp.where(sel, kmeta, 0.0).astype(BF16)
        s = lax.dot_general(q, kh, nt, preferred_element_type=F32) + bias_ref[0, h]
        sm = lax.dot_general(q, kmh, nt, preferred_element_type=F32)
        mx = jnp.maximum(jnp.max(s, -1, keepdims=True), jnp.max(sm, -1, keepdims=True))
        p = jnp.exp(s - mx)
        pm = jnp.exp(sm - mx)
        denom = jnp.sum(p, -1, keepdims=True) + jnp.sum(pm, -1, keepdims=True)
        o = (jnp.dot(p.astype(BF16), vwin, preferred_element_type=F32)
             + jnp.dot(pm.astype(BF16), vmeta, preferred_element_type=F32)) / denom
        out = jnp.where(sel, o, out)
    o_ref[0] = out


def _natten(proj3, proj_meta, bias_tab):
    bsz, t_len, _ = proj3.shape
    rows = t_len // GRID_W
    assert rows % NAT_ROWS == 0 and rows >= NAT_KROWS
    nb = rows // NAT_ROWS
    nq = NAT_ROWS * GRID_W
    hp = NH_B // 2
    qb, kb, vb = OFF_QB // LANES, OFF_KB // LANES, OFF_VB // LANES

    def pat(i):
        return jnp.where(i == 0, 0, jnp.where(i == nb - 1, 2, 1))

    return pl.pallas_call(
        _natten_kernel,
        grid=(bsz, hp, nb),
        in_specs=[
            pl.BlockSpec((1, nq, LANES), lambda b, p, i: (b, i, qb + p)),
            pl.BlockSpec((1, t_len, LANES), lambda b, p, i: (b, 0, kb + p)),
            pl.BlockSpec((1, t_len, LANES), lambda b, p, i: (b, 0, vb + p)),
            pl.BlockSpec((N_META, LANES), lambda b, p, i: (0, kb + p)),
            pl.BlockSpec((N_META, LANES), lambda b, p, i: (0, vb + p)),
            pl.BlockSpec((1, 2, nq, NAT_KROWS * GRID_W), lambda b, p, i: (pat(i), p, 0, 0)),
        ],
        out_specs=pl.BlockSpec((1, nq, LANES), lambda b, p, i: (b, i, p)),
        out_shape=jax.ShapeDtypeStruct((bsz, t_len, D_B), F32),
        compiler_params=_cparams(("arbitrary", "arbitrary", "arbitrary")),
        name="natten",
    )(proj3, proj3, proj3, proj_meta, proj_meta, bias_tab)


def _merge_kernel(hf_ref, hb_ref, oa_ref, ga_ref, gb_ref, yb_ref, hg_ref, wa_ref, wb_ref, o_ref, ya_scr):
    for hd in range(NH_A):
        sl = slice(hd * DV_A, (hd + 1) * DV_A)
        h = hf_ref[:, sl] + hb_ref[:, sl]
        hc = h - jnp.mean(h, -1, keepdims=True)
        var = jnp.mean(hc * hc, -1, keepdims=True)
        hn = hc * lax.rsqrt(var + LN_EPS) * hg_ref[:, sl]
        ya_scr[:, sl] = (hn * jax.nn.sigmoid(oa_ref[:, sl])).astype(BF16)
    a = jnp.dot(ya_scr[...], wa_ref[...], preferred_element_type=F32)
    b = jnp.dot(yb_ref[...].astype(BF16), wb_ref[...], preferred_element_type=F32)
    o_ref[...] = (jax.nn.sigmoid(ga_ref[...]) * a + jax.nn.sigmoid(gb_ref[...]) * b).astype(BF16)


def _merge(h_f, h_b, proj, y_b, head_g, w_a, w_b):
    n = h_f.shape[0]
    tm = MERGE_TM
    const = lambda i: (0, 0)
    return pl.pallas_call(
        _merge_kernel,
        grid=(n // tm,),
        in_specs=[
            pl.BlockSpec((tm, D_A), lambda i: (i, 0)),
            pl.BlockSpec((tm, D_A), lambda i: (i, 0)),
            pl.BlockSpec((tm, D_A), lambda i: (i, OFF_OA // D_A)),
            pl.BlockSpec((tm, D_MODEL), lambda i: (i, OFF_GA // D_MODEL)),
            pl.BlockSpec((tm, D_MODEL), lambda i: (i, OFF_GB // D_MODEL)),
            pl.BlockSpec((tm, D_B), lambda i: (i, 0)),
            pl.BlockSpec((1, D_A), const),
            pl.BlockSpec((D_A, D_MODEL), const, pipeline_mode=pl.Buffered(1)),
            pl.BlockSpec((D_B, D_MODEL), const, pipeline_mode=pl.Buffered(1)),
        ],
        out_specs=pl.BlockSpec((tm, D_MODEL), lambda i: (i, 0)),
        out_shape=jax.ShapeDtypeStruct((n, D_MODEL), BF16),
        scratch_shapes=[pltpu.VMEM((tm, D_A), BF16)],
        compiler_params=_cparams(("arbitrary",)),
        name="merge",
    )(h_f, h_b, proj, proj, proj, y_b, head_g, w_a, w_b)


def _outproj_router_kernel(m_ref, x_ref, lg_ref, lb_ref, wo_ref, g1_ref, b1_ref, wr_hi_ref, wr_lo_ref, br_ref,
                           h1_ref, route_ref):
    mix = jnp.dot(m_ref[...], wo_ref[...], preferred_element_type=F32)
    h0 = _layer_norm(x_ref[...], lg_ref[...], lb_ref[...])
    h1 = _layer_norm(ALPHA * h0 + mix, g1_ref[...], b1_ref[...])
    h1_ref[...] = h1

    hi = h1.astype(BF16)
    lo = (h1 - hi.astype(F32)).astype(BF16)
    d = functools.partial(jnp.dot, preferred_element_type=F32)
    logits = d(hi, wr_hi_ref[...]) + (d(lo, wr_hi_ref[...]) + d(hi, wr_lo_ref[...])) + br_ref[...]

    lane = lax.broadcasted_iota(jnp.int32, logits.shape, 1)
    big = jnp.int32(LANES)
    is_g = lane < N_GROUPS
    gl = jnp.where(is_g, logits, NEG_INF)
    gmax = jnp.max(gl, -1, keepdims=True)
    gsel = jnp.min(jnp.where(is_g & (gl == gmax), lane, big), -1, keepdims=True)
    pg = 1.0 / jnp.sum(jnp.where(is_g, jnp.exp(gl - gmax), 0.0), -1, keepdims=True)
    e_lo = N_GROUPS + gsel * EXPERTS_PER_GROUP
    in_grp = (lane >= e_lo) & (lane < e_lo + EXPERTS_PER_GROUP)
    el = jnp.where(in_grp, logits, NEG_INF)
    v0 = jnp.max(el, -1, keepdims=True)
    i0 = jnp.min(jnp.where(in_grp & (el == v0), lane, big), -1, keepdims=True)
    el1 = jnp.where(lane == i0, NEG_INF, el)
    v1 = jnp.max(el1, -1, keepdims=True)
    i1 = jnp.min(jnp.where(in_grp & (lane != i0) & (el1 == v1), lane, big), -1, keepdims=True)
    e1 = jnp.exp(v1 - v0)
    den = 1.0 + e1
    g0 = pg * (1.0 / den)
    g1 = pg * (e1 / den)
    out = jnp.where(lane == 0, (i0 - N_GROUPS).astype(F32),
                    jnp.where(lane == 1, (i1 - N_GROUPS).astype(F32),
                              jnp.where(lane == 2, g0, jnp.where(lane == 3, g1, 0.0))))
    route_ref[...] = out


def _outproj_router(merged, x, ln_in_g, ln_in_b, w_out, ln1_g, ln1_b, wr_hi, wr_lo, br):
    n = x.shape[0]
    tm = OUT_TM
    const = lambda i: (0, 0)
    row = lambda i: (i, 0)
    return pl.pallas_call(
        _outproj_router_kernel,
        grid=(n // tm,),
        in_specs=[
            pl.BlockSpec((tm, D_MODEL), row),
            pl.BlockSpec((tm, D_MODEL), row),
            pl.BlockSpec((1, D_MODEL), const),
            pl.BlockSpec((1, D_MODEL), const),
            pl.BlockSpec((D_MODEL, D_MODEL), const, pipeline_mode=pl.Buffered(1)),
            pl.BlockSpec((1, D_MODEL), const),
            pl.BlockSpec((1, D_MODEL), const),
            pl.BlockSpec((D_MODEL, LANES), const),
            pl.BlockSpec((D_MODEL, LANES), const),
            pl.BlockSpec((1, LANES), const),
        ],
        out_specs=[pl.BlockSpec((tm, D_MODEL), row), pl.BlockSpec((tm, LANES), row)],
        out_shape=[jax.ShapeDtypeStruct((n, D_MODEL), F32), jax.ShapeDtypeStruct((n, LANES), F32)],
        compiler_params=_cparams(("arbitrary",)),
        name="outproj_router",
    )(merged, x, ln_in_g, ln_in_b, w_out, ln1_g, ln1_b, wr_hi, wr_lo, br)


def _moe_ffn_kernel(be_ref, nu_ref, tok_ref, tokn_ref, h1_hbm, wgu_ref, wd_ref, y_ref, xbuf, sem):
    i = pl.program_id(0)
    n_used = nu_ref[0]
    bm = xbuf.shape[1]
    slot = lax.rem(i, 2)
    nxt = 1 - slot

    def start_rows(t_ref, dst_slot, r0, r1):
        for r in range(r0, r1):
            pltpu.make_async_copy(h1_hbm.at[pl.ds(t_ref[0, 0, r], 1)], xbuf.at[dst_slot, pl.ds(r, 1)],
                                  sem.at[dst_slot]).start()

    def wait_rows(s):
        pltpu.make_async_copy(h1_hbm.at[pl.ds(0, bm)], xbuf.at[s], sem.at[s]).wait()

    @pl.when(i == 0)
    def _():
        start_rows(tok_ref, 0, 0, bm)

    @pl.when(i < n_used)
    def _():
        wait_rows(slot)
        xb = xbuf[slot].astype(BF16)
        nt = (((1,), (1,)), ((), ()))
        q = bm // 4
        start_rows(tokn_ref, nxt, 0, q)
        g_t = lax.dot_general(wgu_ref[0, :D_EXPERT], xb, nt, preferred_element_type=F32)
        start_rows(tokn_ref, nxt, q, 2 * q)
        u_t = lax.dot_general(wgu_ref[0, D_EXPERT:], xb, nt, preferred_element_type=F32)
        h_t = (jax.nn.silu(g_t) * u_t).astype(BF16)
        for half in range(2):
            start_rows(tokn_ref, nxt, (2 + half) * q, (3 + half) * q)
            sl = slice(half * D_EXPERT, (half + 1) * D_EXPERT)
            y_ref[:, sl] = jnp.dot(wd_ref[0, sl], h_t, preferred_element_type=F32).T

    @pl.when(i >= n_used)
    def _():
        @pl.when(i == n_used)
        def _():
            wait_rows(slot)

        y_ref[...] = jnp.zeros(y_ref.shape, F32)


def _moe_ffn(block_e, n_used, tok3, h1, w_gu_t, w_down_t):
    nb = tok3.shape[0]
    bm = MOE_BM
    wmap = lambda i, be, nu: (be[i], 0, 0)
    grid_spec = pltpu.PrefetchScalarGridSpec(
        num_scalar_prefetch=2,
        grid=(nb,),
        in_specs=[
            pl.BlockSpec((1, 1, bm), lambda i, be, nu: (i, 0, 0), memory_space=pltpu.SMEM),
            pl.BlockSpec((1, 1, bm), lambda i, be, nu: (jnp.minimum(i + 1, nb - 1), 0, 0),
                         memory_space=pltpu.SMEM),
            pl.BlockSpec(memory_space=pl.ANY),
            pl.BlockSpec((1, 2 * D_EXPERT, D_MODEL), wmap),
            pl.BlockSpec((1, D_MODEL, D_EXPERT), wmap),
        ],
        out_specs=pl.BlockSpec((bm, D_MODEL), lambda i, be, nu: (i, 0)),
        scratch_shapes=[pltpu.VMEM((2, bm, D_MODEL), F32), pltpu.SemaphoreType.DMA((2,))],
    )
    return pl.pallas_call(
        _moe_ffn_kernel,
        grid_spec=grid_spec,
        out_shape=jax.ShapeDtypeStruct((nb * bm, D_MODEL), F32),
        compiler_params=_cparams(("arbitrary",)),
        name="moe_ffn",
    )(block_e, n_used, tok3, tok3, h1, w_gu_t, w_down_t)


def _combine_kernel(pos_ref, posn_ref, y_hbm, h1_ref, route_ref, g_ref, b_ref, o_ref, buf, sem):
    i = pl.program_id(0)
    tm = buf.shape[2]
    slot = lax.rem(i, 2)

    def start_rows(p_ref, s):
        for r in range(tm):
            for k in range(2):
                pltpu.make_async_copy(y_hbm.at[pl.ds(p_ref[0, 0, 2 * r + k], 1)],
                                      buf.at[s, k, pl.ds(r, 1)], sem.at[s]).start()

    @pl.when(i == 0)
    def _():
        start_rows(pos_ref, 0)

    @pl.when(i + 1 < pl.num_programs(0))
    def _():
        start_rows(posn_ref, 1 - slot)

    for k in range(2):
        pltpu.make_async_copy(y_hbm.at[pl.ds(0, tm)], buf.at[slot, k], sem.at[slot]).wait()
    route = route_ref[...]
    ffn = buf[slot, 0] * route[:, 2:3] + buf[slot, 1] * route[:, 3:4]
    o_ref[...] = _layer_norm(ALPHA * h1_ref[...] + ffn, g_ref[...], b_ref[...])


def _combine(pos3, y_sorted, h1, route, ln2_g, ln2_b):
    n = h1.shape[0]
    tm = COMB_TM
    nt = n // tm
    const = lambda i: (0, 0)
    row = lambda i: (i, 0)
    return pl.pallas_call(
        _combine_kernel,
        grid=(nt,),
        in_specs=[
            pl.BlockSpec((1, 1, 2 * tm), lambda i: (i, 0, 0), memory_space=pltpu.SMEM),
            pl.BlockSpec((1, 1, 2 * tm), lambda i: (jnp.minimum(i + 1, nt - 1), 0, 0), memory_space=pltpu.SMEM),
            pl.BlockSpec(memory_space=pl.ANY),
            pl.BlockSpec((tm, D_MODEL), row),
            pl.BlockSpec((tm, LANES), row),
            pl.BlockSpec((1, D_MODEL), const),
            pl.BlockSpec((1, D_MODEL), const),
        ],
        out_specs=pl.BlockSpec((tm, D_MODEL), row),
        out_shape=jax.ShapeDtypeStruct((n, D_MODEL), F32),
        scratch_shapes=[pltpu.VMEM((2, 2, tm, D_MODEL), F32), pltpu.SemaphoreType.DMA((2,))],
        compiler_params=_cparams(("arbitrary",)),
        name="combine",
    )(pos3, pos3, y_sorted, h1, route, ln2_g, ln2_b)


def _dispatch_plan(route, n):
    bm = MOE_BM
    a = 2 * n
    nb = a // bm + N_EXPERTS
    flat_e = route[:, 0:2].astype(jnp.int32).reshape(a)
    onehot = (flat_e[:, None] == jnp.arange(N_EXPERTS, dtype=jnp.int32)[None, :]).astype(jnp.int32)
    csum = jnp.cumsum(onehot, axis=0)
    rank = jnp.take_along_axis(csum, flat_e[:, None], axis=1)[:, 0] - 1
    counts = csum[-1]
    pcounts = (counts + bm - 1) // bm * bm
    pend = jnp.cumsum(pcounts)
    pstart = pend - pcounts
    pos = pstart[flat_e] + rank
    tok = jnp.zeros((nb * bm,), jnp.int32).at[pos].set(jnp.arange(a, dtype=jnp.int32) // 2)
    block_e = jnp.minimum(
        jnp.searchsorted(pend, jnp.arange(nb, dtype=jnp.int32) * bm, side="right"), N_EXPERTS - 1
    ).astype(jnp.int32)
    n_used = (pend[-1:] // bm).astype(jnp.int32)
    return block_e, n_used, tok.reshape(nb, 1, bm), pos.reshape(n // COMB_TM, 1, 2 * COMB_TM)


def _encode(x, p):
    bsz, t_len, _ = x.shape
    n = bsz * t_len
    x2 = x.reshape(n, D_MODEL)
    proj, gates = _inproj(x2, p["ln_in_g"], p["ln_in_b"], p["w_main"], p["w_gates"])
    proj3 = proj.reshape(bsz, t_len, P_MAIN)
    gates3 = gates.reshape(bsz, t_len, LANES)
    gates_row = gates3[:, :, :4 * NH_A].transpose(0, 2, 1)
    qk, k_meta = _conv(proj3, p["proj_meta"], p["conv_w"], p["conv_b"])
    h_f, h_b = _mlstm(qk, proj3, gates3, gates_row, k_meta, p["proj_meta"], p["gates_meta"],
                      p["bias_col"], p["bias_row"])
    y_b = _natten(proj3, p["proj_meta"], p["bias_tab"])
    merged = _merge(h_f.reshape(n, D_A), h_b.reshape(n, D_A), proj, y_b.reshape(n, D_B),
                    p["head_g"], p["w_a"], p["w_b"])
    h1, route = _outproj_router(merged, x2, p["ln_in_g"], p["ln_in_b"], p["w_out"], p["ln1_g"], p["ln1_b"],
                                p["wr_hi"], p["wr_lo"], p["br"])
    block_e, n_used, tok3, pos3 = _dispatch_plan(route, n)
    y_sorted = _moe_ffn(block_e, n_used, tok3, h1, p["w_gu_t"], p["w_down_t"])
    out = _combine(pos3, y_sorted, h1, route, p["ln2_g"], p["ln2_b"])
    return out.reshape(bsz, t_len, D_MODEL)


def kernel(x_prompt, x_sample, meta_tokens, ln_in_g, ln_in_b, w_in, b_gates, conv_w, conv_b, head_g, rpb, w_a, w_b, w_out, ln1_g, ln1_b, router_g_w, router_g_b, router_e_w, router_e_b, w_gate, w_up, w_down, ln2_g, ln2_b):
    l = 0
    offs = np.cumsum((0,) + IN_SPLITS)
    col = lambda i: w_in[l][:, offs[i]:offs[i + 1]]
    w_main = jnp.concatenate([col(0), col(1), col(2), col(3), col(8), col(9), col(5), col(6), col(7)],
                             axis=1).astype(BF16)
    w_gates = jnp.pad(col(4), ((0, 0), (0, LANES - 4 * NH_A))).astype(BF16)
    row = lambda v: v.reshape(1, -1).astype(F32)
    wr = jnp.pad(jnp.concatenate([router_g_w[l], router_e_w[l]], axis=1),
                 ((0, 0), (0, LANES - N_GROUPS - N_EXPERTS)))
    wr_hi = wr.astype(BF16)
    p = {
        "ln_in_g": row(ln_in_g), "ln_in_b": row(ln_in_b),
        "w_main": w_main, "w_gates": w_gates,
        "conv_w": conv_w[l], "conv_b": row(conv_b[l]),
        "bias_col": jnp.pad(b_gates[l], (0, LANES - 4 * NH_A)).reshape(1, LANES),
        "bias_row": b_gates[l].reshape(4 * NH_A, 1),
        "head_g": row(head_g[l]),
        "bias_tab": _natten_bias_table(rpb[l]),
        "w_a": w_a[l].astype(BF16), "w_b": w_b[l].astype(BF16), "w_out": w_out[l].astype(BF16),
        "ln1_g": row(ln1_g[l]), "ln1_b": row(ln1_b[l]),
        "wr_hi": wr_hi, "wr_lo": (wr - wr_hi.astype(F32)).astype(BF16),
        "br": jnp.pad(jnp.concatenate([router_g_b[l], router_e_b[l]]),
                      (0, LANES - N_GROUPS - N_EXPERTS)).reshape(1, LANES),
        "w_gu_t": jnp.concatenate([w_gate[l], w_up[l]], axis=2).astype(BF16).transpose(0, 2, 1),
        "w_down_t": w_down[l].astype(BF16).transpose(0, 2, 1),
        "ln2_g": row(ln2_g[l]), "ln2_b": row(ln2_b[l]),
    }
    p["proj_meta"], p["gates_meta"] = _inproj(meta_tokens, p["ln_in_g"], p["ln_in_b"], w_main, w_gates)
    return (_encode(x_prompt, p), _encode(x_sample, p))
```

```python
import functools

import jax
import jax.numpy as jnp
import numpy as np
from jax import lax
from jax.experimental import pallas as pl
from jax.experimental.pallas import tpu as pltpu

F32 = jnp.float32
BF16 = jnp.bfloat16

D_MODEL = 2048
N_META = 16
GRID_W = 64
NH_A = 8
DK_A = 128
DV_A = 256
QK_A = NH_A * DK_A
D_A = NH_A * DV_A
NH_B = 16
DH_B = 64
D_B = NH_B * DH_B
KR_MAX = 8
KC = 16
Q_COL_BLOCK = 16
K_COL_SPAN = 32
N_GROUPS = 4
EXPERTS_PER_GROUP = 8
N_EXPERTS = N_GROUPS * EXPERTS_PER_GROUP
D_EXPERT = 1024
DEPTH = 1
ALPHA = (2 * DEPTH) ** 0.25
LN_EPS = 1e-5
NEG_INF = -1e30
IN_SPLITS = (QK_A, QK_A, D_A, D_A, 4 * NH_A, D_B, D_B, D_B, D_MODEL, D_MODEL)

OFF_QK = 0
OFF_VA = 2048
OFF_OA = 4096
OFF_GA = 6144
OFF_GB = 8192
OFF_QB = 10240
OFF_KB = 11264
OFF_VB = 12288
P_MAIN = 13312
LANES = 128

VMEM_LIMIT = 56 * 1024 * 1024

INPROJ_TM = 1024
INPROJ_TN = 1024
INPROJ_LN_SLAB = 256
CONV_TC = 512
MLSTM_CHUNK = 256
NAT_ROWS = 8
NAT_KROWS = 16
MERGE_TM = 256
OUT_TM = 512
MOE_BM = 256
COMB_TM = 256
RANK_TM = 512


def _cparams(sem):
    return pltpu.CompilerParams(dimension_semantics=sem, vmem_limit_bytes=VMEM_LIMIT)


def _layer_norm(x, g, b):
    xc = x - jnp.mean(x, -1, keepdims=True)
    var = jnp.mean(xc * xc, -1, keepdims=True)
    return xc * lax.rsqrt(var + LN_EPS) * g + b


def _inproj_kernel(x_ref, g_ref, b_ref, w_ref, wg_ref, o_ref, og_ref, h_scr):
    @pl.when(pl.program_id(1) == 0)
    def _():
        tm = x_ref.shape[0]
        slab = min(INPROJ_LN_SLAB, tm)

        def ln_slab(s, carry):
            r = pl.multiple_of(s * slab, slab)
            h = _layer_norm(x_ref[pl.ds(r, slab), :], g_ref[...], b_ref[...]).astype(BF16)
            h_scr[pl.ds(r, slab), :] = h
            og_ref[pl.ds(r, slab), :] = jnp.dot(h, wg_ref[...], preferred_element_type=F32)
            return carry

        lax.fori_loop(0, tm // slab, ln_slab, 0)

    o_ref[...] = jnp.dot(h_scr[...], w_ref[...], preferred_element_type=F32)


def _inproj(x, ln_g, ln_b, w_main, w_gates):
    n = x.shape[0]
    tm = min(INPROJ_TM, n)
    tn = INPROJ_TN
    return pl.pallas_call(
        _inproj_kernel,
        grid=(n // tm, P_MAIN // tn),
        in_specs=[
            pl.BlockSpec((tm, D_MODEL), lambda i, j: (i, 0)),
            pl.BlockSpec((1, D_MODEL), lambda i, j: (0, 0)),
            pl.BlockSpec((1, D_MODEL), lambda i, j: (0, 0)),
            pl.BlockSpec((D_MODEL, tn), lambda i, j: (0, j)),
            pl.BlockSpec((D_MODEL, LANES), lambda i, j: (0, 0)),
        ],
        out_specs=[
            pl.BlockSpec((tm, tn), lambda i, j: (i, j)),
            pl.BlockSpec((tm, LANES), lambda i, j: (i, 0)),
        ],
        out_shape=[
            jax.ShapeDtypeStruct((n, P_MAIN), F32),
            jax.ShapeDtypeStruct((n, LANES), F32),
        ],
        scratch_shapes=[pltpu.VMEM((tm, D_MODEL), BF16)],
        compiler_params=_cparams(("arbitrary", "arbitrary")),
        name="inproj",
    )(x, ln_g, ln_b, w_main, w_gates)


def _conv_kernel(x_ref, prev_ref, next_ref, meta_ref, w_ref, b_ref, qk_ref, kmeta_ref):
    t = pl.program_id(1)
    last = pl.num_programs(1) - 1
    tc = x_ref.shape[1]
    w0, w1, w2 = w_ref[0:1, :], w_ref[1:2, :], w_ref[2:3, :]
    bias = b_ref[...]
    lane = lax.broadcasted_iota(jnp.int32, (1, 2 * QK_A), 1)
    out_scale = jnp.where(lane >= QK_A, DK_A ** -0.5, 1.0).astype(F32)

    x = x_ref[0]
    meta = meta_ref[...]
    prev_row = jnp.where(t == 0, meta[N_META - 1:N_META, :], prev_ref[0, 7:8, :])
    next_row = jnp.where(t == last, jnp.zeros_like(prev_row), next_ref[0, 0:1, :])
    row = lax.broadcasted_iota(jnp.int32, (tc, 1), 0)
    x_prev = jnp.where(row == 0, prev_row, pltpu.roll(x, 1, 0))
    x_next = jnp.where(row == tc - 1, next_row, pltpu.roll(x, tc - 1, 0))
    y = x_prev * w0 + x * w1 + x_next * w2 + bias
    qk_ref[0] = (jax.nn.silu(y) * out_scale).astype(BF16)

    @pl.when(t == 0)
    def _():
        mrow = lax.broadcasted_iota(jnp.int32, (N_META, 1), 0)
        m_prev = jnp.where(mrow == 0, 0.0, pltpu.roll(meta, 1, 0))
        m_next = jnp.where(mrow == N_META - 1, x[0:1, :], pltpu.roll(meta, N_META - 1, 0))
        ym = m_prev * w0 + meta * w1 + m_next * w2 + bias
        km = jax.nn.silu(ym) * out_scale
        kmeta_ref[0] = km[:, QK_A:].astype(BF16)


def _conv(proj3, proj_meta, conv_w, conv_b):
    bsz, t_len, _ = proj3.shape
    tc = CONV_TC
    nt = t_len // tc
    r8 = tc // 8
    return pl.pallas_call(
        _conv_kernel,
        grid=(bsz, nt),
        in_specs=[
            pl.BlockSpec((1, tc, 2 * QK_A), lambda b, t: (b, t, 0)),
            pl.BlockSpec((1, 8, 2 * QK_A), lambda b, t: (b, jnp.maximum(t * r8 - 1, 0), 0)),
            pl.BlockSpec((1, 8, 2 * QK_A), lambda b, t: (b, jnp.minimum((t + 1) * r8, t_len // 8 - 1), 0)),
            pl.BlockSpec((N_META, 2 * QK_A), lambda b, t: (0, 0)),
            pl.BlockSpec((3, 2 * QK_A), lambda b, t: (0, 0)),
            pl.BlockSpec((1, 2 * QK_A), lambda b, t: (0, 0)),
        ],
        out_specs=[
            pl.BlockSpec((1, tc, 2 * QK_A), lambda b, t: (b, t, 0)),
            pl.BlockSpec((1, N_META, QK_A), lambda b, t: (b, 0, 0)),
        ],
        out_shape=[
            jax.ShapeDtypeStruct((bsz, t_len, 2 * QK_A), BF16),
            jax.ShapeDtypeStruct((bsz, N_META, QK_A), BF16),
        ],
        compiler_params=_cparams(("arbitrary", "arbitrary")),
        name="conv_silu",
    )(proj3, proj3, proj3, proj_meta, conv_w, conv_b)


def _split3(x):
    hi = x.astype(BF16)
    r1 = x - hi.astype(F32)
    mid = r1.astype(BF16)
    lo = (r1 - mid.astype(F32)).astype(BF16)
    return hi, mid, lo


def _tri_left(tri, x):
    hi, mid, lo = _split3(x)
    d = functools.partial(jnp.dot, preferred_element_type=F32)
    return d(tri, hi) + d(tri, mid) + d(tri, lo)


def _tri_right(x, tri):
    hi, mid, lo = _split3(x)
    d = functools.partial(jnp.dot, preferred_element_type=F32)
    return d(hi, tri) + d(mid, tri) + d(lo, tri)


def _state_update(k, v, ig_col, b_col, b_last, c_old, n_old, m_old):
    logw = b_last - b_col + ig_col
    m_new = jnp.maximum(b_last + m_old, jnp.max(logw, axis=0, keepdims=True))
    w = jnp.exp(logw - m_new)
    decay = jnp.exp(b_last + m_old - m_new)
    kw = k.astype(F32) * w
    kv = lax.dot_general(kw.astype(BF16), v, (((0,), (0,)), ((), ())), preferred_element_type=F32)
    c_new = decay * c_old + kv
    n_new = decay * n_old + jnp.sum(kw, axis=0, keepdims=True)
    return c_new, n_new, m_new


def _mlstm_kernel(qkf_ref, qkb_ref, vf_ref, vb_ref, gcf_ref, gcb_ref, grf_ref, grb_ref,
                  kmeta_ref, vmeta_ref, gmeta_ref, bcol_ref, brow_ref,
                  hf_ref, hb_ref, c_scr, n_scr, m_scr):
    j = pl.program_id(1)
    c = qkf_ref.shape[1]
    r_i = lax.broadcasted_iota(jnp.int32, (c, c), 0)
    c_i = lax.broadcasted_iota(jnp.int32, (c, c), 1)
    lower = r_i >= c_i
    upper = r_i <= c_i
    tri_l = lower.astype(BF16)
    tri_u = upper.astype(BF16)

    @pl.when(j == 0)
    def _():
        gm = gmeta_ref[...] + bcol_ref[...]
        lfm = jax.nn.log_sigmoid(gm)
        mr = lax.broadcasted_iota(jnp.int32, (N_META, N_META), 0)
        mc = lax.broadcasted_iota(jnp.int32, (N_META, N_META), 1)
        bm = _tri_left((mr >= mc).astype(BF16), lfm)
        zero_m = jnp.zeros((1, 1), F32)
        for hd in range(NH_A):
            ig_col = gm[:, hd:hd + 1]
            b_col = bm[:, 2 * NH_A + hd:2 * NH_A + hd + 1]
            b_last = b_col[N_META - 1:N_META, :]
            k = kmeta_ref[0, :, hd * DK_A:(hd + 1) * DK_A]
            v = vmeta_ref[:, hd * DV_A:(hd + 1) * DV_A].astype(BF16)
            c_new, n_new, m_new = _state_update(
                k, v, ig_col, b_col, b_last,
                jnp.zeros((DK_A, DV_A), F32), jnp.zeros((1, DK_A), F32), zero_m)
            c_scr[hd] = c_new
            n_scr[hd:hd + 1, :] = n_new
            m_scr[hd:hd + 1, :] = jnp.broadcast_to(m_new, (1, LANES))
        for hd in range(NH_A):
            u = NH_A + hd
            c_scr[u] = jnp.zeros((DK_A, DV_A), F32)
            n_scr[u:u + 1, :] = jnp.zeros((1, DK_A), F32)
            m_scr[u:u + 1, :] = jnp.zeros((1, LANES), F32)

    n_all = n_scr[...]
    m_all = m_scr[...]
    nt = (((1,), (1,)), ((), ()))
    units = []
    for direction in range(2):
        rev = direction == 1
        qk_ref = qkb_ref if rev else qkf_ref
        v_ref = vb_ref if rev else vf_ref
        for hd in range(NH_A):
            u = direction * NH_A + hd
            q = qk_ref[0, :, hd * DK_A:(hd + 1) * DK_A]
            k = qk_ref[0, :, QK_A + hd * DK_A:QK_A + (hd + 1) * DK_A]
            v = v_ref[0, :, hd * DV_A:(hd + 1) * DV_A].astype(BF16)
            c_old = c_scr[u]
            units.append(dict(
                rev=rev, hd=hd, u=u, q=q, k=k, v=v, c_old=c_old,
                n_old=n_all[u:u + 1, :], m_old=m_all[u:u + 1, 0:1],
                s_raw=lax.dot_general(q, k, nt, preferred_element_type=F32),
                q_c=jnp.dot(q, c_old.astype(BF16), preferred_element_type=F32)))

    for direction in range(2):
        rev = direction == 1
        gc = (gcb_ref if rev else gcf_ref)[0] + bcol_ref[...]
        gr = (grb_ref if rev else grf_ref)[0] + brow_ref[...]
        lf_c = jax.nn.log_sigmoid(gc)
        lf_r = jax.nn.log_sigmoid(gr)
        if rev:
            bc_all = _tri_left(tri_u, lf_c)
            br_all = _tri_right(lf_r, tri_l)
            mask = upper
        else:
            bc_all = _tri_left(tri_l, lf_c)
            br_all = _tri_right(lf_r, tri_u)
            mask = lower
        for hd in range(NH_A):
            un = units[direction * NH_A + hd]
            ig_lane = direction * NH_A + hd
            lf_lane = 2 * NH_A + direction * NH_A + hd
            ig_col = gc[:, ig_lane:ig_lane + 1]
            b_col = bc_all[:, lf_lane:lf_lane + 1]
            ig_row = gr[ig_lane:ig_lane + 1, :]
            b_row = br_all[lf_lane:lf_lane + 1, :]
            m_old = un["m_old"]
            logd = jnp.where(mask, b_col + (ig_row - b_row), NEG_INF)
            log_inter = b_col + m_old
            m_t = jnp.maximum(log_inter, jnp.max(logd, axis=-1, keepdims=True))
            s = un["s_raw"] * jnp.exp(logd - m_t)
            w_inter = jnp.exp(log_inter - m_t)
            q_n = jnp.sum(un["q"].astype(F32) * un["n_old"], axis=-1, keepdims=True)
            den = jnp.sum(s, axis=-1, keepdims=True) + w_inter * q_n
            b_last = b_col[0:1, :] if rev else b_col[c - 1:c, :]
            logw = b_last - b_col + ig_col
            m_new = jnp.maximum(b_last + m_old, jnp.max(logw, axis=0, keepdims=True))
            kw = un["k"].astype(F32) * jnp.exp(logw - m_new)
            un.update(s=s.astype(BF16), w_inter=w_inter, kw=kw, m_new=m_new,
                      inv=1.0 / jnp.maximum(jnp.abs(den), jnp.exp(-m_t)),
                      decay=jnp.exp(b_last + m_old - m_new))

    tt = (((0,), (0,)), ((), ()))
    for un in units:
        un["sv"] = jnp.dot(un["s"], un["v"], preferred_element_type=F32)
        un["kv"] = lax.dot_general(un["kw"].astype(BF16), un["v"], tt, preferred_element_type=F32)

    n_out, m_out = [], []
    for un in units:
        out_ref = hb_ref if un["rev"] else hf_ref
        hd = un["hd"]
        out_ref[0, :, hd * DV_A:(hd + 1) * DV_A] = (un["sv"] + un["w_inter"] * un["q_c"]) * un["inv"]
        c_scr[un["u"]] = un["decay"] * un["c_old"] + un["kv"]
        n_out.append(un["decay"] * un["n_old"] + jnp.sum(un["kw"], axis=0, keepdims=True))
        m_out.append(jnp.broadcast_to(un["m_new"], (1, LANES)))
    n_scr[...] = jnp.concatenate(n_out, axis=0)
    m_scr[...] = jnp.concatenate(m_out, axis=0)


def _mlstm(qk, proj3, gates_col, gates_row, k_meta, proj_meta, gates_meta, bias_col, bias_row):
    bsz, t_len, _ = qk.shape
    c = MLSTM_CHUNK
    nc = t_len // c
    va_blk = OFF_VA // D_A
    fwd = lambda b, j: (b, j, 0)
    bwd = lambda b, j: (b, nc - 1 - j, 0)
    return pl.pallas_call(
        _mlstm_kernel,
        grid=(bsz, nc),
        in_specs=[
            pl.BlockSpec((1, c, 2 * QK_A), fwd),
            pl.BlockSpec((1, c, 2 * QK_A), bwd),
            pl.BlockSpec((1, c, D_A), lambda b, j: (b, j, va_blk)),
            pl.BlockSpec((1, c, D_A), lambda b, j: (b, nc - 1 - j, va_blk)),
            pl.BlockSpec((1, c, LANES), fwd),
            pl.BlockSpec((1, c, LANES), bwd),
            pl.BlockSpec((1, 4 * NH_A, c), lambda b, j: (b, 0, j)),
            pl.BlockSpec((1, 4 * NH_A, c), lambda b, j: (b, 0, nc - 1 - j)),
            pl.BlockSpec((1, N_META, QK_A), lambda b, j: (b, 0, 0)),
            pl.BlockSpec((N_META, D_A), lambda b, j: (0, va_blk)),
            pl.BlockSpec((N_META, LANES), lambda b, j: (0, 0)),
            pl.BlockSpec((1, LANES), lambda b, j: (0, 0)),
            pl.BlockSpec((4 * NH_A, 1), lambda b, j: (0, 0)),
        ],
        out_specs=[
            pl.BlockSpec((1, c, D_A), fwd),
            pl.BlockSpec((1, c, D_A), bwd),
        ],
        out_shape=[
            jax.ShapeDtypeStruct((bsz, t_len, D_A), F32),
            jax.ShapeDtypeStruct((bsz, t_len, D_A), F32),
        ],
        scratch_shapes=[
            pltpu.VMEM((2 * NH_A, DK_A, DV_A), F32),
            pltpu.VMEM((2 * NH_A, DK_A), F32),
            pltpu.VMEM((2 * NH_A, LANES), F32),
        ],
        compiler_params=_cparams(("arbitrary", "arbitrary")),
        name="mlstm",
    )(qk, qk, proj3, proj3, gates_col, gates_col, gates_row, gates_row,
      k_meta, proj_meta, gates_meta, bias_col, bias_row)


def _natten_col_blocks():
    cols = np.arange(GRID_W)
    cstart = np.clip(cols - KC // 2, 0, GRID_W - KC)
    ustart = np.clip(cstart[::Q_COL_BLOCK], 0, GRID_W - K_COL_SPAN)
    return cols, cstart, ustart


def _natten_bias_table(rpb):
    cols, cstart, ustart = _natten_col_blocks()
    ncb = GRID_W // Q_COL_BLOCK
    qcols = cols.reshape(ncb, Q_COL_BLOCK)[:, :, None]
    kcols = (ustart[:, None] + np.arange(K_COL_SPAN)[None, :])[:, None, :]
    qs = cstart.reshape(ncb, Q_COL_BLOCK)[:, :, None]
    col_valid = (kcols >= qs) & (kcols < qs + KC)
    col_idx = np.clip(kcols - qcols + KC - 1, 0, 2 * KC - 2)
    qr = np.arange(NAT_ROWS)[:, None]
    kk = np.arange(NAT_KROWS)[None, :]
    half = KR_MAX // 2
    rel_start = np.stack([np.maximum(qr - half, 0) + 0 * kk,
                          qr + 0 * kk,
                          np.minimum(qr + half, NAT_KROWS - KR_MAX) + 0 * kk])
    row_delta = np.stack([kk - qr, kk - qr - half, kk - qr - (NAT_KROWS - NAT_ROWS)])
    row_valid = (kk[None] >= rel_start) & (kk[None] < rel_start + KR_MAX)
    row_idx = np.clip(row_delta + KR_MAX - 1, 0, 2 * KR_MAX - 2)
    rpb_c = jnp.where(col_valid[None, None], rpb[:, :, col_idx], NEG_INF)
    tab = rpb_c[:, row_idx]
    tab = jnp.where(row_valid[None, :, :, :, None, None, None], tab, NEG_INF)
    tab = tab.transpose(1, 0, 4, 2, 5, 3, 6)
    return tab.reshape(3, NH_B, ncb, NAT_ROWS * Q_COL_BLOCK, NAT_KROWS * K_COL_SPAN)


def _natten_kernel(q_ref, k_ref, v_ref, km_ref, vm_ref, bias_ref, o_ref):
    i = pl.program_id(2)
    rows = k_ref.shape[1]
    ks = jnp.clip(i * NAT_ROWS - KR_MAX // 2, 0, rows - NAT_KROWS)
    _, _, ustart = _natten_col_blocks()
    nq = NAT_ROWS * Q_COL_BLOCK
    nk = NAT_KROWS * K_COL_SPAN
    lane = lax.broadcasted_iota(jnp.int32, (1, LANES), 1)
    kmeta = km_ref[...]
    vmeta = vm_ref[...].astype(BF16)
    nt = (((1,), (1,)), ((), ()))
    sels = [(lane >= h * DH_B) & (lane < (h + 1) * DH_B) for h in range(2)]
    kmh = [jnp.where(sel, kmeta, 0.0).astype(BF16) for sel in sels]
    ncb = GRID_W // Q_COL_BLOCK
    probs = []
    for j in range(ncb):
        u = int(ustart[j])
        qc = slice(j * Q_COL_BLOCK, (j + 1) * Q_COL_BLOCK)
        q = (q_ref[0, :, qc, :].reshape(nq, LANES) * (DH_B ** -0.5)).astype(BF16)
        kwin = k_ref[0, pl.ds(ks, NAT_KROWS), u:u + K_COL_SPAN, :].reshape(nk, LANES)
        vwin = v_ref[0, pl.ds(ks, NAT_KROWS), u:u + K_COL_SPAN, :].reshape(nk, LANES).astype(BF16)
        for h in range(2):
            kh = jnp.where(sels[h], kwin, 0.0).astype(BF16)
            probs.append(dict(
                j=j, h=h, vwin=vwin,
                s=lax.dot_general(q, kh, nt, preferred_element_type=F32),
                sm=lax.dot_general(q, kmh[h], nt, preferred_element_type=F32)))
    for pr in probs:
        s = pr["s"] + bias_ref[0, pr["h"], pr["j"]]
        sm = pr["sm"]
        mx = jnp.maximum(jnp.max(s, -1, keepdims=True), jnp.max(sm, -1, keepdims=True))
        p = jnp.exp(s - mx)
        pm = jnp.exp(sm - mx)
        pr.update(p=p.astype(BF16), pm=pm.astype(BF16),
                  inv=1.0 / (jnp.sum(p, -1, keepdims=True) + jnp.sum(pm, -1, keepdims=True)))
    for pr in probs:
        pr["o"] = (jnp.dot(pr["p"], pr["vwin"], preferred_element_type=F32)
                   + jnp.dot(pr["pm"], vmeta, preferred_element_type=F32)) * pr["inv"]
    for j in range(ncb):
        qc = slice(j * Q_COL_BLOCK, (j + 1) * Q_COL_BLOCK)
        out = jnp.where(sels[0], probs[2 * j]["o"], probs[2 * j + 1]["o"])
        o_ref[0, :, qc, :] = out.reshape(NAT_ROWS, Q_COL_BLOCK, LANES)


def _natten(proj3, proj_meta, bias_tab):
    bsz, t_len, _ = proj3.shape
    rows = t_len // GRID_W
    assert rows % NAT_ROWS == 0 and rows >= NAT_KROWS
    nb = rows // NAT_ROWS
    hp = NH_B // 2
    ncb = GRID_W // Q_COL_BLOCK
    qb, kb, vb = OFF_QB // LANES, OFF_KB // LANES, OFF_VB // LANES
    proj4 = proj3.reshape(bsz, rows, GRID_W, P_MAIN)

    def pat(i):
        return jnp.where(i == 0, 0, jnp.where(i == nb - 1, 2, 1))

    out = pl.pallas_call(
        _natten_kernel,
        grid=(bsz, hp, nb),
        in_specs=[
            pl.BlockSpec((1, NAT_ROWS, GRID_W, LANES), lambda b, p, i: (b, i, 0, qb + p)),
            pl.BlockSpec((1, rows, GRID_W, LANES), lambda b, p, i: (b, 0, 0, kb + p)),
            pl.BlockSpec((1, rows, GRID_W, LANES), lambda b, p, i: (b, 0, 0, vb + p)),
            pl.BlockSpec((N_META, LANES), lambda b, p, i: (0, kb + p)),
            pl.BlockSpec((N_META, LANES), lambda b, p, i: (0, vb + p)),
            pl.BlockSpec((1, 2, ncb, NAT_ROWS * Q_COL_BLOCK, NAT_KROWS * K_COL_SPAN),
                         lambda b, p, i: (pat(i), p, 0, 0, 0)),
        ],
        out_specs=pl.BlockSpec((1, NAT_ROWS, GRID_W, LANES), lambda b, p, i: (b, i, 0, p)),
        out_shape=jax.ShapeDtypeStruct((bsz, rows, GRID_W, D_B), F32),
        compiler_params=_cparams(("arbitrary", "arbitrary", "arbitrary")),
        name="natten",
    )(proj4, proj4, proj4, proj_meta, proj_meta, bias_tab)
    return out.reshape(bsz, t_len, D_B)


def _merge_kernel(hf_ref, hb_ref, oa_ref, ga_ref, gb_ref, yb_ref, hg_ref, wa_ref, wb_ref, o_ref, ya_scr):
    for hd in range(NH_A):
        sl = slice(hd * DV_A, (hd + 1) * DV_A)
        h = hf_ref[:, sl] + hb_ref[:, sl]
        hc = h - jnp.mean(h, -1, keepdims=True)
        var = jnp.mean(hc * hc, -1, keepdims=True)
        hn = hc * lax.rsqrt(var + LN_EPS) * hg_ref[:, sl]
        ya_scr[:, sl] = (hn * jax.nn.sigmoid(oa_ref[:, sl])).astype(BF16)
    a = jnp.dot(ya_scr[...], wa_ref[...], preferred_element_type=F32)
    b = jnp.dot(yb_ref[...].astype(BF16), wb_ref[...], preferred_element_type=F32)
    o_ref[...] = (jax.nn.sigmoid(ga_ref[...]) * a + jax.nn.sigmoid(gb_ref[...]) * b).astype(BF16)


def _merge(h_f, h_b, proj, y_b, head_g, w_a, w_b):
    n = h_f.shape[0]
    tm = MERGE_TM
    const = lambda i: (0, 0)
    return pl.pallas_call(
        _merge_kernel,
        grid=(n // tm,),
        in_specs=[
            pl.BlockSpec((tm, D_A), lambda i: (i, 0)),
            pl.BlockSpec((tm, D_A), lambda i: (i, 0)),
            pl.BlockSpec((tm, D_A), lambda i: (i, OFF_OA // D_A)),
            pl.BlockSpec((tm, D_MODEL), lambda i: (i, OFF_GA // D_MODEL)),
            pl.BlockSpec((tm, D_MODEL), lambda i: (i, OFF_GB // D_MODEL)),
            pl.BlockSpec((tm, D_B), lambda i: (i, 0)),
            pl.BlockSpec((1, D_A), const),
            pl.BlockSpec((D_A, D_MODEL), const, pipeline_mode=pl.Buffered(1)),
            pl.BlockSpec((D_B, D_MODEL), const, pipeline_mode=pl.Buffered(1)),
        ],
        out_specs=pl.BlockSpec((tm, D_MODEL), lambda i: (i, 0)),
        out_shape=jax.ShapeDtypeStruct((n, D_MODEL), BF16),
        scratch_shapes=[pltpu.VMEM((tm, D_A), BF16)],
        compiler_params=_cparams(("arbitrary",)),
        name="merge",
    )(h_f, h_b, proj, proj, proj, y_b, head_g, w_a, w_b)


def _outproj_router_kernel(m_ref, x_ref, lg_ref, lb_ref, wo_ref, g1_ref, b1_ref, wr_hi_ref, wr_lo_ref, br_ref,
                           h1_ref, route_ref):
    mix = jnp.dot(m_ref[...], wo_ref[...], preferred_element_type=F32)
    h0 = _layer_norm(x_ref[...], lg_ref[...], lb_ref[...])
    h1 = _layer_norm(ALPHA * h0 + mix, g1_ref[...], b1_ref[...])
    h1_ref[...] = h1

    hi = h1.astype(BF16)
    lo = (h1 - hi.astype(F32)).astype(BF16)
    d = functools.partial(jnp.dot, preferred_element_type=F32)
    logits = d(hi, wr_hi_ref[...]) + (d(lo, wr_hi_ref[...]) + d(hi, wr_lo_ref[...])) + br_ref[...]

    lane = lax.broadcasted_iota(jnp.int32, logits.shape, 1)
    big = jnp.int32(LANES)
    is_g = lane < N_GROUPS
    gl = jnp.where(is_g, logits, NEG_INF)
    gmax = jnp.max(gl, -1, keepdims=True)
    gsel = jnp.min(jnp.where(is_g & (gl == gmax), lane, big), -1, keepdims=True)
    pg = 1.0 / jnp.sum(jnp.where(is_g, jnp.exp(gl - gmax), 0.0), -1, keepdims=True)
    e_lo = N_GROUPS + gsel * EXPERTS_PER_GROUP
    in_grp = (lane >= e_lo) & (lane < e_lo + EXPERTS_PER_GROUP)
    el = jnp.where(in_grp, logits, NEG_INF)
    v0 = jnp.max(el, -1, keepdims=True)
    i0 = jnp.min(jnp.where(in_grp & (el == v0), lane, big), -1, keepdims=True)
    el1 = jnp.where(lane == i0, NEG_INF, el)
    v1 = jnp.max(el1, -1, keepdims=True)
    i1 = jnp.min(jnp.where(in_grp & (lane != i0) & (el1 == v1), lane, big), -1, keepdims=True)
    e1 = jnp.exp(v1 - v0)
    den = 1.0 + e1
    g0 = pg * (1.0 / den)
    g1 = pg * (e1 / den)
    out = jnp.where(lane == 0, (i0 - N_GROUPS).astype(F32),
                    jnp.where(lane == 1, (i1 - N_GROUPS).astype(F32),
                              jnp.where(lane == 2, g0, jnp.where(lane == 3, g1, 0.0))))
    route_ref[...] = out


def _outproj_router(merged, x, ln_in_g, ln_in_b, w_out, ln1_g, ln1_b, wr_hi, wr_lo, br):
    n = x.shape[0]
    tm = OUT_TM
    const = lambda i: (0, 0)
    row = lambda i: (i, 0)
    return pl.pallas_call(
        _outproj_router_kernel,
        grid=(n // tm,),
        in_specs=[
            pl.BlockSpec((tm, D_MODEL), row),
            pl.BlockSpec((tm, D_MODEL), row),
            pl.BlockSpec((1, D_MODEL), const),
            pl.BlockSpec((1, D_MODEL), const),
            pl.BlockSpec((D_MODEL, D_MODEL), const, pipeline_mode=pl.Buffered(1)),
            pl.BlockSpec((1, D_MODEL), const),
            pl.BlockSpec((1, D_MODEL), const),
            pl.BlockSpec((D_MODEL, LANES), const),
            pl.BlockSpec((D_MODEL, LANES), const),
            pl.BlockSpec((1, LANES), const),
        ],
        out_specs=[pl.BlockSpec((tm, D_MODEL), row), pl.BlockSpec((tm, LANES), row)],
        out_shape=[jax.ShapeDtypeStruct((n, D_MODEL), F32), jax.ShapeDtypeStruct((n, LANES), F32)],
        compiler_params=_cparams(("arbitrary",)),
        name="outproj_router",
    )(merged, x, ln_in_g, ln_in_b, w_out, ln1_g, ln1_b, wr_hi, wr_lo, br)


def _moe_ffn_kernel(be_ref, nu_ref, tok_ref, tokn_ref, h1_hbm, wg_ref, wu_ref, wd_ref, y_ref, xbuf, sem):
    i = pl.program_id(0)
    n_used = nu_ref[0]
    bm = xbuf.shape[1]
    slot = lax.rem(i, 2)
    nxt = 1 - slot

    def start_rows(t_ref, dst_slot, r0, r1):
        for r in range(r0, r1):
            pltpu.make_async_copy(h1_hbm.at[pl.ds(t_ref[0, 0, r], 1)], xbuf.at[dst_slot, pl.ds(r, 1)],
                                  sem.at[dst_slot]).start()

    def wait_rows(s):
        pltpu.make_async_copy(h1_hbm.at[pl.ds(0, bm)], xbuf.at[s], sem.at[s]).wait()

    @pl.when(i == 0)
    def _():
        start_rows(tok_ref, 0, 0, bm)

    @pl.when(i < n_used)
    def _():
        wait_rows(slot)
        xb = xbuf[slot].astype(BF16)
        tn = (((0,), (1,)), ((), ()))
        tt = (((0,), (0,)), ((), ()))
        q = bm // 4
        start_rows(tokn_ref, nxt, 0, q)
        g_t = lax.dot_general(wg_ref[0], xb, tn, preferred_element_type=F32)
        start_rows(tokn_ref, nxt, q, 2 * q)
        u_t = lax.dot_general(wu_ref[0], xb, tn, preferred_element_type=F32)
        h_t = (jax.nn.silu(g_t) * u_t).astype(BF16)
        for half in range(2):
            start_rows(tokn_ref, nxt, (2 + half) * q, (3 + half) * q)
            sl = slice(half * D_EXPERT, (half + 1) * D_EXPERT)
            y_ref[:, sl] = lax.dot_general(wd_ref[0, :, sl], h_t, tt, preferred_element_type=F32).T

    @pl.when(i >= n_used)
    def _():
        @pl.when(i == n_used)
        def _():
            wait_rows(slot)

        y_ref[...] = jnp.zeros(y_ref.shape, F32)


def _moe_ffn(block_e, n_used, tok3, h1, w_gate, w_up, w_down):
    nb = tok3.shape[0]
    bm = MOE_BM
    wmap = lambda i, be, nu: (be[i], 0, 0)
    grid_spec = pltpu.PrefetchScalarGridSpec(
        num_scalar_prefetch=2,
        grid=(nb,),
        in_specs=[
            pl.BlockSpec((1, 1, bm), lambda i, be, nu: (i, 0, 0), memory_space=pltpu.SMEM),
            pl.BlockSpec((1, 1, bm), lambda i, be, nu: (jnp.minimum(i + 1, nb - 1), 0, 0),
                         memory_space=pltpu.SMEM),
            pl.BlockSpec(memory_space=pl.ANY),
            pl.BlockSpec((1, D_MODEL, D_EXPERT), wmap),
            pl.BlockSpec((1, D_MODEL, D_EXPERT), wmap),
            pl.BlockSpec((1, D_EXPERT, D_MODEL), wmap),
        ],
        out_specs=pl.BlockSpec((bm, D_MODEL), lambda i, be, nu: (i, 0)),
        scratch_shapes=[pltpu.VMEM((2, bm, D_MODEL), F32), pltpu.SemaphoreType.DMA((2,))],
    )
    return pl.pallas_call(
        _moe_ffn_kernel,
        grid_spec=grid_spec,
        out_shape=jax.ShapeDtypeStruct((nb * bm, D_MODEL), F32),
        compiler_params=_cparams(("arbitrary",)),
        name="moe_ffn",
    )(block_e, n_used, tok3, tok3, h1, w_gate, w_up, w_down)


def _combine_kernel(pos_ref, posn_ref, y_hbm, h1_ref, route_ref, g_ref, b_ref, o_ref, buf, sem):
    i = pl.program_id(0)
    tm = buf.shape[2]
    slot = lax.rem(i, 2)

    def start_rows(p_ref, s):
        for r in range(tm):
            for k in range(2):
                pltpu.make_async_copy(y_hbm.at[pl.ds(p_ref[0, 0, 2 * r + k], 1)],
                                      buf.at[s, k, pl.ds(r, 1)], sem.at[s]).start()

    @pl.when(i == 0)
    def _():
        start_rows(pos_ref, 0)

    @pl.when(i + 1 < pl.num_programs(0))
    def _():
        start_rows(posn_ref, 1 - slot)

    for k in range(2):
        pltpu.make_async_copy(y_hbm.at[pl.ds(0, tm)], buf.at[slot, k], sem.at[slot]).wait()
    route = route_ref[...]
    ffn = buf[slot, 0] * route[:, 2:3] + buf[slot, 1] * route[:, 3:4]
    o_ref[...] = _layer_norm(ALPHA * h1_ref[...] + ffn, g_ref[...], b_ref[...])


def _combine(pos3, y_sorted, h1, route, ln2_g, ln2_b):
    n = h1.shape[0]
    tm = COMB_TM
    nt = n // tm
    const = lambda i: (0, 0)
    row = lambda i: (i, 0)
    return pl.pallas_call(
        _combine_kernel,
        grid=(nt,),
        in_specs=[
            pl.BlockSpec((1, 1, 2 * tm), lambda i: (i, 0, 0), memory_space=pltpu.SMEM),
            pl.BlockSpec((1, 1, 2 * tm), lambda i: (jnp.minimum(i + 1, nt - 1), 0, 0), memory_space=pltpu.SMEM),
            pl.BlockSpec(memory_space=pl.ANY),
            pl.BlockSpec((tm, D_MODEL), row),
            pl.BlockSpec((tm, LANES), row),
            pl.BlockSpec((1, D_MODEL), const),
            pl.BlockSpec((1, D_MODEL), const),
        ],
        out_specs=pl.BlockSpec((tm, D_MODEL), row),
        out_shape=jax.ShapeDtypeStruct((n, D_MODEL), F32),
        scratch_shapes=[pltpu.VMEM((2, 2, tm, D_MODEL), F32), pltpu.SemaphoreType.DMA((2,))],
        compiler_params=_cparams(("arbitrary",)),
        name="combine",
    )(pos3, pos3, y_sorted, h1, route, ln2_g, ln2_b)


def _rank_kernel(route_ref, pstart_ref, pos_ref, carry):
    @pl.when(pl.program_id(0) == 0)
    def _():
        carry[...] = jnp.zeros(carry.shape, F32)

    tm = route_ref.shape[0]
    route = route_ref[...]
    lane = lax.broadcasted_iota(jnp.int32, (tm, LANES), 1)
    oh0 = lane == route[:, 0:1].astype(jnp.int32)
    oh1 = lane == route[:, 1:2].astype(jnp.int32)
    cnt = jnp.where(oh0, 1.0, 0.0) + jnp.where(oh1, 1.0, 0.0)
    r_i = lax.broadcasted_iota(jnp.int32, (tm, tm), 0)
    c_i = lax.broadcasted_iota(jnp.int32, (tm, tm), 1)
    earlier = jnp.dot((c_i < r_i).astype(BF16), cnt.astype(BF16), preferred_element_type=F32)
    row = earlier + (carry[...] + pstart_ref[...])
    pos0 = jnp.sum(jnp.where(oh0, row, 0.0), -1, keepdims=True)
    pos1 = jnp.sum(jnp.where(oh1, row, 0.0), -1, keepdims=True)
    pos_ref[...] = jnp.where(lane == 0, pos0, jnp.where(lane == 1, pos1, 0.0)).astype(jnp.int32)
    carry[...] += jnp.sum(cnt, axis=0, keepdims=True)


def _rank(route, pstart_row):
    n = route.shape[0]
    tm = RANK_TM
    return pl.pallas_call(
        _rank_kernel,
        grid=(n // tm,),
        in_specs=[pl.BlockSpec((tm, LANES), lambda i: (i, 0)), pl.BlockSpec((1, LANES), lambda i: (0, 0))],
        out_specs=pl.BlockSpec((tm, LANES), lambda i: (i, 0)),
        out_shape=jax.ShapeDtypeStruct((n, LANES), jnp.int32),
        scratch_shapes=[pltpu.VMEM((1, LANES), F32)],
        compiler_params=_cparams(("arbitrary",)),
        name="rank",
    )(route, pstart_row)


def _dispatch_plan(route, n):
    bm = MOE_BM
    a = 2 * n
    nb = a // bm + N_EXPERTS
    id_bits = max(a - 1, 1).bit_length()
    flat_e = route[:, 0:2].astype(jnp.int32).reshape(a)
    ids = jnp.arange(a, dtype=jnp.int32)
    order = jnp.sort(flat_e * (1 << id_bits) + ids) & ((1 << id_bits) - 1)
    experts = jnp.arange(N_EXPERTS, dtype=jnp.int32)
    counts = jnp.sum((flat_e[:, None] == experts[None, :]).astype(jnp.int32), axis=0)
    pcounts = (counts + bm - 1) // bm * bm
    incl = (experts[:, None] <= experts[None, :]).astype(jnp.int32)
    pend = pcounts @ incl
    pstart = pend - pcounts
    cstart = counts @ incl - counts
    blk0 = jnp.arange(nb, dtype=jnp.int32) * bm
    block_e = jnp.minimum(jnp.sum((pend[None, :] <= blk0[:, None]).astype(jnp.int32), axis=1),
                          N_EXPERTS - 1)
    off = (blk0 - pstart[block_e])[:, None] + jnp.arange(bm, dtype=jnp.int32)[None, :]
    valid = off < counts[block_e][:, None]
    src = jnp.clip(cstart[block_e][:, None] + off, 0, a - 1)
    tok = jnp.where(valid, order[src] >> 1, 0)
    n_used = (pend[-1:] // bm).astype(jnp.int32)
    pstart_row = jnp.pad(pstart.astype(F32), (0, LANES - N_EXPERTS)).reshape(1, LANES)
    pos = _rank(route, pstart_row)[:, :2]
    return (block_e.astype(jnp.int32), n_used, tok.reshape(nb, 1, bm),
            pos.reshape(n // COMB_TM, 1, 2 * COMB_TM))


def _encode(x, p):
    bsz, t_len, _ = x.shape
    n = bsz * t_len
    x2 = x.reshape(n, D_MODEL)
    proj, gates = _inproj(x2, p["ln_in_g"], p["ln_in_b"], p["w_main"], p["w_gates"])
    proj3 = proj.reshape(bsz, t_len, P_MAIN)
    gates3 = gates.reshape(bsz, t_len, LANES)
    gates_row = gates3[:, :, :4 * NH_A].transpose(0, 2, 1)
    qk, k_meta = _conv(proj3, p["proj_meta"], p["conv_w"], p["conv_b"])
    h_f, h_b = _mlstm(qk, proj3, gates3, gates_row, k_meta, p["proj_meta"], p["gates_meta"],
                      p["bias_col"], p["bias_row"])
    y_b = _natten(proj3, p["proj_meta"], p["bias_tab"])
    merged = _merge(h_f.reshape(n, D_A), h_b.reshape(n, D_A), proj, y_b.reshape(n, D_B),
                    p["head_g"], p["w_a"], p["w_b"])
    h1, route = _outproj_router(merged, x2, p["ln_in_g"], p["ln_in_b"], p["w_out"], p["ln1_g"], p["ln1_b"],
                                p["wr_hi"], p["wr_lo"], p["br"])
    block_e, n_used, tok3, pos3 = _dispatch_plan(route, n)
    y_sorted = _moe_ffn(block_e, n_used, tok3, h1, p["w_gate"], p["w_up"], p["w_down"])
    out = _combine(pos3, y_sorted, h1, route, p["ln2_g"], p["ln2_b"])
    return out.reshape(bsz, t_len, D_MODEL)


def kernel(x_prompt, x_sample, meta_tokens, ln_in_g, ln_in_b, w_in, b_gates, conv_w, conv_b, head_g, rpb, w_a, w_b, w_out, ln1_g, ln1_b, router_g_w, router_g_b, router_e_w, router_e_b, w_gate, w_up, w_down, ln2_g, ln2_b):
    l = 0
    offs = np.cumsum((0,) + IN_SPLITS)
    col = lambda i: w_in[l][:, offs[i]:offs[i + 1]]
    w_main = jnp.concatenate([col(0), col(1), col(2), col(3), col(8), col(9), col(5), col(6), col(7)],
                             axis=1).astype(BF16)
    w_gates = jnp.pad(col(4), ((0, 0), (0, LANES - 4 * NH_A))).astype(BF16)
    row = lambda v: v.reshape(1, -1).astype(F32)
    wr = jnp.pad(jnp.concatenate([router_g_w[l], router_e_w[l]], axis=1),
                 ((0, 0), (0, LANES - N_GROUPS - N_EXPERTS)))
    wr_hi = wr.astype(BF16)
    p = {
        "ln_in_g": row(ln_in_g), "ln_in_b": row(ln_in_b),
        "w_main": w_main, "w_gates": w_gates,
        "conv_w": conv_w[l], "conv_b": row(conv_b[l]),
        "bias_col": jnp.pad(b_gates[l], (0, LANES - 4 * NH_A)).reshape(1, LANES),
        "bias_row": b_gates[l].reshape(4 * NH_A, 1),
        "head_g": row(head_g[l]),
        "bias_tab": _natten_bias_table(rpb[l]),
        "w_a": w_a[l].astype(BF16), "w_b": w_b[l].astype(BF16), "w_out": w_out[l].astype(BF16),
        "ln1_g": row(ln1_g[l]), "ln1_b": row(ln1_b[l]),
        "wr_hi": wr_hi, "wr_lo": (wr - wr_hi.astype(F32)).astype(BF16),
        "br": jnp.pad(jnp.concatenate([router_g_b[l], router_e_b[l]]),
                      (0, LANES - N_GROUPS - N_EXPERTS)).reshape(1, LANES),
        "w_gate": w_gate[l].astype(BF16), "w_up": w_up[l].astype(BF16), "w_down": w_down[l].astype(BF16),
        "ln2_g": row(ln2_g[l]), "ln2_b": row(ln2_b[l]),
    }
    p["proj_meta"], p["gates_meta"] = _inproj(meta_tokens, p["ln_in_g"], p["ln_in_b"], w_main, w_gates)
    return (_encode(x_prompt, p), _encode(x_sample, p))
```

```python
import functools

import jax
import jax.numpy as jnp
import numpy as np
from jax import lax
from jax.experimental import pallas as pl
from jax.experimental.pallas import tpu as pltpu

F32 = jnp.float32
BF16 = jnp.bfloat16

D_MODEL = 2048
N_META = 16
GRID_W = 64
NH_A = 8
DK_A = 128
DV_A = 256
QK_A = NH_A * DK_A
D_A = NH_A * DV_A
NH_B = 16
DH_B = 64
D_B = NH_B * DH_B
KR_MAX = 8
KC = 16
Q_COL_BLOCK = 16
K_COL_SPAN = 32
N_GROUPS = 4
EXPERTS_PER_GROUP = 8
N_EXPERTS = N_GROUPS * EXPERTS_PER_GROUP
D_EXPERT = 1024
DEPTH = 1
ALPHA = (2 * DEPTH) ** 0.25
LN_EPS = 1e-5
NEG_INF = -1e30
IN_SPLITS = (QK_A, QK_A, D_A, D_A, 4 * NH_A, D_B, D_B, D_B, D_MODEL, D_MODEL)

OFF_QK = 0
OFF_VA = 2048
OFF_OA = 4096
OFF_GA = 6144
OFF_GB = 8192
OFF_QB = 10240
OFF_KB = 11264
OFF_VB = 12288
P_MAIN = 13312
LANES = 128

VMEM_LIMIT = 56 * 1024 * 1024

INPROJ_TM = 1024
INPROJ_TN = 1024
INPROJ_LN_SLAB = 256
CONV_TC = 512
MLSTM_CHUNK = 256
NAT_ROWS = 8
NAT_KROWS = 16
MERGE_TM = 256
OUT_TM = 512
OUT_SUBTILES = 2
MOE_BM = 256
COMB_TM = 256
RANK_TM = 512


def _cparams(sem):
    return pltpu.CompilerParams(dimension_semantics=sem, vmem_limit_bytes=VMEM_LIMIT)


def _layer_norm(x, g, b):
    xc = x - jnp.mean(x, -1, keepdims=True)
    var = jnp.mean(xc * xc, -1, keepdims=True)
    return xc * lax.rsqrt(var + LN_EPS) * g + b


def _inproj_kernel(x_ref, g_ref, b_ref, w_ref, wg_ref, o_ref, og_ref, h_scr):
    @pl.when(pl.program_id(1) == 0)
    def _():
        tm = x_ref.shape[0]
        slab = min(INPROJ_LN_SLAB, tm)

        def ln_slab(s, carry):
            r = pl.multiple_of(s * slab, slab)
            h = _layer_norm(x_ref[pl.ds(r, slab), :], g_ref[...], b_ref[...]).astype(BF16)
            h_scr[pl.ds(r, slab), :] = h
            og_ref[pl.ds(r, slab), :] = jnp.dot(h, wg_ref[...], preferred_element_type=F32)
            return carry

        lax.fori_loop(0, tm // slab, ln_slab, 0)

    o_ref[...] = jnp.dot(h_scr[...], w_ref[...], preferred_element_type=F32)


def _inproj(x, ln_g, ln_b, w_main, w_gates):
    n = x.shape[0]
    tm = min(INPROJ_TM, n)
    tn = INPROJ_TN
    return pl.pallas_call(
        _inproj_kernel,
        grid=(n // tm, P_MAIN // tn),
        in_specs=[
            pl.BlockSpec((tm, D_MODEL), lambda i, j: (i, 0)),
            pl.BlockSpec((1, D_MODEL), lambda i, j: (0, 0)),
            pl.BlockSpec((1, D_MODEL), lambda i, j: (0, 0)),
            pl.BlockSpec((D_MODEL, tn), lambda i, j: (0, j)),
            pl.BlockSpec((D_MODEL, LANES), lambda i, j: (0, 0)),
        ],
        out_specs=[
            pl.BlockSpec((tm, tn), lambda i, j: (i, j)),
            pl.BlockSpec((tm, LANES), lambda i, j: (i, 0)),
        ],
        out_shape=[
            jax.ShapeDtypeStruct((n, P_MAIN), F32),
            jax.ShapeDtypeStruct((n, LANES), F32),
        ],
        scratch_shapes=[pltpu.VMEM((tm, D_MODEL), BF16)],
        compiler_params=_cparams(("arbitrary", "arbitrary")),
        name="inproj",
    )(x, ln_g, ln_b, w_main, w_gates)


def _conv_kernel(x_ref, prev_ref, next_ref, meta_ref, w_ref, b_ref, qk_ref, kmeta_ref):
    t = pl.program_id(1)
    last = pl.num_programs(1) - 1
    tc = x_ref.shape[1]
    w0, w1, w2 = w_ref[0:1, :], w_ref[1:2, :], w_ref[2:3, :]
    bias = b_ref[...]
    lane = lax.broadcasted_iota(jnp.int32, (1, 2 * QK_A), 1)
    out_scale = jnp.where(lane >= QK_A, DK_A ** -0.5, 1.0).astype(F32)

    x = x_ref[0]
    meta = meta_ref[...]
    prev_row = jnp.where(t == 0, meta[N_META - 1:N_META, :], prev_ref[0, 7:8, :])
    next_row = jnp.where(t == last, jnp.zeros_like(prev_row), next_ref[0, 0:1, :])
    row = lax.broadcasted_iota(jnp.int32, (tc, 1), 0)
    x_prev = jnp.where(row == 0, prev_row, pltpu.roll(x, 1, 0))
    x_next = jnp.where(row == tc - 1, next_row, pltpu.roll(x, tc - 1, 0))
    y = x_prev * w0 + x * w1 + x_next * w2 + bias
    qk_ref[0] = (jax.nn.silu(y) * out_scale).astype(BF16)

    @pl.when(t == 0)
    def _():
        mrow = lax.broadcasted_iota(jnp.int32, (N_META, 1), 0)
        m_prev = jnp.where(mrow == 0, 0.0, pltpu.roll(meta, 1, 0))
        m_next = jnp.where(mrow == N_META - 1, x[0:1, :], pltpu.roll(meta, N_META - 1, 0))
        ym = m_prev * w0 + meta * w1 + m_next * w2 + bias
        km = jax.nn.silu(ym) * out_scale
        kmeta_ref[0] = km[:, QK_A:].astype(BF16)


def _conv(proj3, proj_meta, conv_w, conv_b):
    bsz, t_len, _ = proj3.shape
    tc = CONV_TC
    nt = t_len // tc
    r8 = tc // 8
    return pl.pallas_call(
        _conv_kernel,
        grid=(bsz, nt),
        in_specs=[
            pl.BlockSpec((1, tc, 2 * QK_A), lambda b, t: (b, t, 0)),
            pl.BlockSpec((1, 8, 2 * QK_A), lambda b, t: (b, jnp.maximum(t * r8 - 1, 0), 0)),
            pl.BlockSpec((1, 8, 2 * QK_A), lambda b, t: (b, jnp.minimum((t + 1) * r8, t_len // 8 - 1), 0)),
            pl.BlockSpec((N_META, 2 * QK_A), lambda b, t: (0, 0)),
            pl.BlockSpec((3, 2 * QK_A), lambda b, t: (0, 0)),
            pl.BlockSpec((1, 2 * QK_A), lambda b, t: (0, 0)),
        ],
        out_specs=[
            pl.BlockSpec((1, tc, 2 * QK_A), lambda b, t: (b, t, 0)),
            pl.BlockSpec((1, N_META, QK_A), lambda b, t: (b, 0, 0)),
        ],
        out_shape=[
            jax.ShapeDtypeStruct((bsz, t_len, 2 * QK_A), BF16),
            jax.ShapeDtypeStruct((bsz, N_META, QK_A), BF16),
        ],
        compiler_params=_cparams(("arbitrary", "arbitrary")),
        name="conv_silu",
    )(proj3, proj3, proj3, proj_meta, conv_w, conv_b)


def _split3(x):
    hi = x.astype(BF16)
    r1 = x - hi.astype(F32)
    mid = r1.astype(BF16)
    lo = (r1 - mid.astype(F32)).astype(BF16)
    return hi, mid, lo


def _tri_left(tri, x):
    hi, mid, lo = _split3(x)
    d = functools.partial(jnp.dot, preferred_element_type=F32)
    return d(tri, hi) + d(tri, mid) + d(tri, lo)


def _tri_right(x, tri):
    hi, mid, lo = _split3(x)
    d = functools.partial(jnp.dot, preferred_element_type=F32)
    return d(hi, tri) + d(mid, tri) + d(lo, tri)


def _state_update(k, v, ig_col, b_col, b_last, c_old, n_old, m_old):
    logw = b_last - b_col + ig_col
    m_new = jnp.maximum(b_last + m_old, jnp.max(logw, axis=0, keepdims=True))
    w = jnp.exp(logw - m_new)
    decay = jnp.exp(b_last + m_old - m_new)
    kw = k.astype(F32) * w
    kv = lax.dot_general(kw.astype(BF16), v, (((0,), (0,)), ((), ())), preferred_element_type=F32)
    c_new = decay * c_old + kv
    n_new = decay * n_old + jnp.sum(kw, axis=0, keepdims=True)
    return c_new, n_new, m_new


def _mlstm_kernel(qkf_ref, qkb_ref, vf_ref, vb_ref, gcf_ref, gcb_ref, grf_ref, grb_ref,
                  kmeta_ref, vmeta_ref, gmeta_ref, bcol_ref, brow_ref,
                  hf_ref, hb_ref, c_scr, n_scr, m_scr):
    j = pl.program_id(1)
    c = qkf_ref.shape[1]
    r_i = lax.broadcasted_iota(jnp.int32, (c, c), 0)
    c_i = lax.broadcasted_iota(jnp.int32, (c, c), 1)
    lower = r_i >= c_i
    upper = r_i <= c_i
    tri_l = lower.astype(BF16)
    tri_u = upper.astype(BF16)

    @pl.when(j == 0)
    def _():
        gm = gmeta_ref[...] + bcol_ref[...]
        lfm = jax.nn.log_sigmoid(gm)
        mr = lax.broadcasted_iota(jnp.int32, (N_META, N_META), 0)
        mc = lax.broadcasted_iota(jnp.int32, (N_META, N_META), 1)
        bm = _tri_left((mr >= mc).astype(BF16), lfm)
        zero_m = jnp.zeros((1, 1), F32)
        for hd in range(NH_A):
            ig_col = gm[:, hd:hd + 1]
            b_col = bm[:, 2 * NH_A + hd:2 * NH_A + hd + 1]
            b_last = b_col[N_META - 1:N_META, :]
            k = kmeta_ref[0, :, hd * DK_A:(hd + 1) * DK_A]
            v = vmeta_ref[:, hd * DV_A:(hd + 1) * DV_A].astype(BF16)
            c_new, n_new, m_new = _state_update(
                k, v, ig_col, b_col, b_last,
                jnp.zeros((DK_A, DV_A), F32), jnp.zeros((1, DK_A), F32), zero_m)
            c_scr[hd] = c_new
            n_scr[hd:hd + 1, :] = n_new
            m_scr[hd:hd + 1, :] = jnp.broadcast_to(m_new, (1, LANES))
        for hd in range(NH_A):
            u = NH_A + hd
            c_scr[u] = jnp.zeros((DK_A, DV_A), F32)
            n_scr[u:u + 1, :] = jnp.zeros((1, DK_A), F32)
            m_scr[u:u + 1, :] = jnp.zeros((1, LANES), F32)

    n_all = n_scr[...]
    m_all = m_scr[...]
    nt = (((1,), (1,)), ((), ()))
    units = []
    for direction in range(2):
        rev = direction == 1
        qk_ref = qkb_ref if rev else qkf_ref
        v_ref = vb_ref if rev else vf_ref
        for hd in range(NH_A):
            u = direction * NH_A + hd
            q = qk_ref[0, :, hd * DK_A:(hd + 1) * DK_A]
            k = qk_ref[0, :, QK_A + hd * DK_A:QK_A + (hd + 1) * DK_A]
            v = v_ref[0, :, hd * DV_A:(hd + 1) * DV_A].astype(BF16)
            c_old = c_scr[u]
            units.append(dict(
                rev=rev, hd=hd, u=u, q=q, k=k, v=v, c_old=c_old,
                n_old=n_all[u:u + 1, :], m_old=m_all[u:u + 1, 0:1],
                s_raw=lax.dot_general(q, k, nt, preferred_element_type=F32),
                q_c=jnp.dot(q, c_old.astype(BF16), preferred_element_type=F32)))

    for direction in range(2):
        rev = direction == 1
        gc = (gcb_ref if rev else gcf_ref)[0] + bcol_ref[...]
        gr = (grb_ref if rev else grf_ref)[0] + brow_ref[...]
        lf_c = jax.nn.log_sigmoid(gc)
        lf_r = jax.nn.log_sigmoid(gr)
        if rev:
            bc_all = _tri_left(tri_u, lf_c)
            br_all = _tri_right(lf_r, tri_l)
            mask = upper
        else:
            bc_all = _tri_left(tri_l, lf_c)
            br_all = _tri_right(lf_r, tri_u)
            mask = lower
        for hd in range(NH_A):
            un = units[direction * NH_A + hd]
            ig_lane = direction * NH_A + hd
            lf_lane = 2 * NH_A + direction * NH_A + hd
            ig_col = gc[:, ig_lane:ig_lane + 1]
            b_col = bc_all[:, lf_lane:lf_lane + 1]
            ig_row = gr[ig_lane:ig_lane + 1, :]
            b_row = br_all[lf_lane:lf_lane + 1, :]
            m_old = un["m_old"]
            logd = jnp.where(mask, b_col + (ig_row - b_row), NEG_INF)
            log_inter = b_col + m_old
            m_t = jnp.maximum(log_inter, jnp.max(logd, axis=-1, keepdims=True))
            s = un["s_raw"] * jnp.exp(logd - m_t)
            w_inter = jnp.exp(log_inter - m_t)
            q_n = jnp.sum(un["q"].astype(F32) * un["n_old"], axis=-1, keepdims=True)
            den = jnp.sum(s, axis=-1, keepdims=True) + w_inter * q_n
            b_last = b_col[0:1, :] if rev else b_col[c - 1:c, :]
            logw = b_last - b_col + ig_col
            m_new = jnp.maximum(b_last + m_old, jnp.max(logw, axis=0, keepdims=True))
            kw = un["k"].astype(F32) * jnp.exp(logw - m_new)
            un.update(s=s.astype(BF16), w_inter=w_inter, kw=kw, m_new=m_new,
                      inv=1.0 / jnp.maximum(jnp.abs(den), jnp.exp(-m_t)),
                      decay=jnp.exp(b_last + m_old - m_new))

    tt = (((0,), (0,)), ((), ()))
    for un in units:
        un["sv"] = jnp.dot(un["s"], un["v"], preferred_element_type=F32)
        un["kv"] = lax.dot_general(un["kw"].astype(BF16), un["v"], tt, preferred_element_type=F32)

    n_out, m_out = [], []
    for un in units:
        out_ref = hb_ref if un["rev"] else hf_ref
        hd = un["hd"]
        out_ref[0, :, hd * DV_A:(hd + 1) * DV_A] = (un["sv"] + un["w_inter"] * un["q_c"]) * un["inv"]
        c_scr[un["u"]] = un["decay"] * un["c_old"] + un["kv"]
        n_out.append(un["decay"] * un["n_old"] + jnp.sum(un["kw"], axis=0, keepdims=True))
        m_out.append(jnp.broadcast_to(un["m_new"], (1, LANES)))
    n_scr[...] = jnp.concatenate(n_out, axis=0)
    m_scr[...] = jnp.concatenate(m_out, axis=0)


def _mlstm(qk, proj3, gates_col, gates_row, k_meta, proj_meta, gates_meta, bias_col, bias_row):
    bsz, t_len, _ = qk.shape
    c = MLSTM_CHUNK
    nc = t_len // c
    va_blk = OFF_VA // D_A
    fwd = lambda b, j: (b, j, 0)
    bwd = lambda b, j: (b, nc - 1 - j, 0)
    return pl.pallas_call(
        _mlstm_kernel,
        grid=(bsz, nc),
        in_specs=[
            pl.BlockSpec((1, c, 2 * QK_A), fwd),
            pl.BlockSpec((1, c, 2 * QK_A), bwd),
            pl.BlockSpec((1, c, D_A), lambda b, j: (b, j, va_blk)),
            pl.BlockSpec((1, c, D_A), lambda b, j: (b, nc - 1 - j, va_blk)),
            pl.BlockSpec((1, c, LANES), fwd),
            pl.BlockSpec((1, c, LANES), bwd),
            pl.BlockSpec((1, 4 * NH_A, c), lambda b, j: (b, 0, j)),
            pl.BlockSpec((1, 4 * NH_A, c), lambda b, j: (b, 0, nc - 1 - j)),
            pl.BlockSpec((1, N_META, QK_A), lambda b, j: (b, 0, 0)),
            pl.BlockSpec((N_META, D_A), lambda b, j: (0, va_blk)),
            pl.BlockSpec((N_META, LANES), lambda b, j: (0, 0)),
            pl.BlockSpec((1, LANES), lambda b, j: (0, 0)),
            pl.BlockSpec((4 * NH_A, 1), lambda b, j: (0, 0)),
        ],
        out_specs=[
            pl.BlockSpec((1, c, D_A), fwd),
            pl.BlockSpec((1, c, D_A), bwd),
        ],
        out_shape=[
            jax.ShapeDtypeStruct((bsz, t_len, D_A), F32),
            jax.ShapeDtypeStruct((bsz, t_len, D_A), F32),
        ],
        scratch_shapes=[
            pltpu.VMEM((2 * NH_A, DK_A, DV_A), F32),
            pltpu.VMEM((2 * NH_A, DK_A), F32),
            pltpu.VMEM((2 * NH_A, LANES), F32),
        ],
        compiler_params=_cparams(("arbitrary", "arbitrary")),
        name="mlstm",
    )(qk, qk, proj3, proj3, gates_col, gates_col, gates_row, gates_row,
      k_meta, proj_meta, gates_meta, bias_col, bias_row)


def _natten_col_blocks():
    cols = np.arange(GRID_W)
    cstart = np.clip(cols - KC // 2, 0, GRID_W - KC)
    ustart = np.clip(cstart[::Q_COL_BLOCK], 0, GRID_W - K_COL_SPAN)
    return cols, cstart, ustart


def _natten_bias_table(rpb):
    cols, cstart, ustart = _natten_col_blocks()
    ncb = GRID_W // Q_COL_BLOCK
    qcols = cols.reshape(ncb, Q_COL_BLOCK)[:, :, None]
    kcols = (ustart[:, None] + np.arange(K_COL_SPAN)[None, :])[:, None, :]
    qs = cstart.reshape(ncb, Q_COL_BLOCK)[:, :, None]
    col_valid = (kcols >= qs) & (kcols < qs + KC)
    col_idx = np.clip(kcols - qcols + KC - 1, 0, 2 * KC - 2)
    qr = np.arange(NAT_ROWS)[:, None]
    kk = np.arange(NAT_KROWS)[None, :]
    half = KR_MAX // 2
    rel_start = np.stack([np.maximum(qr - half, 0) + 0 * kk,
                          qr + 0 * kk,
                          np.minimum(qr + half, NAT_KROWS - KR_MAX) + 0 * kk])
    row_delta = np.stack([kk - qr, kk - qr - half, kk - qr - (NAT_KROWS - NAT_ROWS)])
    rpb_c = jnp.where(col_valid[None, None], rpb[:, :, col_idx], NEG_INF)
    rpb_c = rpb_c.transpose(0, 2, 3, 1, 4)
    kinds = []
    for kind in range(3):
        per_row = []
        for r in range(NAT_ROWS):
            first = int(rel_start[kind, r, 0])
            bias_row0 = int(row_delta[kind, r, first]) + KR_MAX - 1
            assert 0 <= bias_row0 and bias_row0 + KR_MAX <= 2 * KR_MAX - 1
            win = rpb_c[:, :, :, bias_row0:bias_row0 + KR_MAX, :]
            per_row.append(jnp.pad(win, ((0, 0), (0, 0), (0, 0), (first, NAT_KROWS - KR_MAX - first), (0, 0)),
                                   constant_values=NEG_INF))
        kinds.append(jnp.stack(per_row, axis=2))
    tab = jnp.stack(kinds, axis=0)
    return tab.reshape(3, NH_B, ncb, NAT_ROWS * Q_COL_BLOCK, NAT_KROWS * K_COL_SPAN)


def _natten_kernel(q_ref, k_ref, v_ref, km_ref, vm_ref, bias_ref, o_ref):
    i = pl.program_id(2)
    rows = k_ref.shape[1]
    ks = jnp.clip(i * NAT_ROWS - KR_MAX // 2, 0, rows - NAT_KROWS)
    _, _, ustart = _natten_col_blocks()
    nq = NAT_ROWS * Q_COL_BLOCK
    nk = NAT_KROWS * K_COL_SPAN
    lane = lax.broadcasted_iota(jnp.int32, (1, LANES), 1)
    kmeta = km_ref[...]
    vmeta = vm_ref[...].astype(BF16)
    nt = (((1,), (1,)), ((), ()))
    sels = [(lane >= h * DH_B) & (lane < (h + 1) * DH_B) for h in range(2)]
    kmh = [jnp.where(sel, kmeta, 0.0).astype(BF16) for sel in sels]
    ncb = GRID_W // Q_COL_BLOCK
    probs = []
    for j in range(ncb):
        u = int(ustart[j])
        qc = slice(j * Q_COL_BLOCK, (j + 1) * Q_COL_BLOCK)
        q = (q_ref[0, :, qc, :].reshape(nq, LANES) * (DH_B ** -0.5)).astype(BF16)
        kwin = k_ref[0, pl.ds(ks, NAT_KROWS), u:u + K_COL_SPAN, :].reshape(nk, LANES)
        vwin = v_ref[0, pl.ds(ks, NAT_KROWS), u:u + K_COL_SPAN, :].reshape(nk, LANES).astype(BF16)
        for h in range(2):
            kh = jnp.where(sels[h], kwin, 0.0).astype(BF16)
            probs.append(dict(
                j=j, h=h, vwin=vwin,
                s=lax.dot_general(q, kh, nt, preferred_element_type=F32),
                sm=lax.dot_general(q, kmh[h], nt, preferred_element_type=F32)))
    for pr in probs:
        s = pr["s"] + bias_ref[0, pr["h"], pr["j"]]
        sm = pr["sm"]
        mx = jnp.maximum(jnp.max(s, -1, keepdims=True), jnp.max(sm, -1, keepdims=True))
        p = jnp.exp(s - mx)
        pm = jnp.exp(sm - mx)
        pr.update(p=p.astype(BF16), pm=pm.astype(BF16),
                  inv=1.0 / (jnp.sum(p, -1, keepdims=True) + jnp.sum(pm, -1, keepdims=True)))
    for pr in probs:
        pr["o"] = (jnp.dot(pr["p"], pr["vwin"], preferred_element_type=F32)
                   + jnp.dot(pr["pm"], vmeta, preferred_element_type=F32)) * pr["inv"]
    for j in range(ncb):
        qc = slice(j * Q_COL_BLOCK, (j + 1) * Q_COL_BLOCK)
        out = jnp.where(sels[0], probs[2 * j]["o"], probs[2 * j + 1]["o"])
        o_ref[0, :, qc, :] = out.reshape(NAT_ROWS, Q_COL_BLOCK, LANES)


def _natten(proj3, proj_meta, bias_tab):
    bsz, t_len, _ = proj3.shape
    rows = t_len // GRID_W
    assert rows % NAT_ROWS == 0 and rows >= NAT_KROWS
    nb = rows // NAT_ROWS
    hp = NH_B // 2
    ncb = GRID_W // Q_COL_BLOCK
    qb, kb, vb = OFF_QB // LANES, OFF_KB // LANES, OFF_VB // LANES
    proj4 = proj3.reshape(bsz, rows, GRID_W, P_MAIN)

    def pat(i):
        return jnp.where(i == 0, 0, jnp.where(i == nb - 1, 2, 1))

    out = pl.pallas_call(
        _natten_kernel,
        grid=(bsz, hp, nb),
        in_specs=[
            pl.BlockSpec((1, NAT_ROWS, GRID_W, LANES), lambda b, p, i: (b, i, 0, qb + p)),
            pl.BlockSpec((1, rows, GRID_W, LANES), lambda b, p, i: (b, 0, 0, kb + p)),
            pl.BlockSpec((1, rows, GRID_W, LANES), lambda b, p, i: (b, 0, 0, vb + p)),
            pl.BlockSpec((N_META, LANES), lambda b, p, i: (0, kb + p)),
            pl.BlockSpec((N_META, LANES), lambda b, p, i: (0, vb + p)),
            pl.BlockSpec((1, 2, ncb, NAT_ROWS * Q_COL_BLOCK, NAT_KROWS * K_COL_SPAN),
                         lambda b, p, i: (pat(i), p, 0, 0, 0)),
        ],
        out_specs=pl.BlockSpec((1, NAT_ROWS, GRID_W, LANES), lambda b, p, i: (b, i, 0, p)),
        out_shape=jax.ShapeDtypeStruct((bsz, rows, GRID_W, D_B), F32),
        compiler_params=_cparams(("arbitrary", "arbitrary", "arbitrary")),
        name="natten",
    )(proj4, proj4, proj4, proj_meta, proj_meta, bias_tab)
    return out.reshape(bsz, t_len, D_B)


def _merge_kernel(hf_ref, hb_ref, oa_ref, ga_ref, gb_ref, yb_ref, hg_ref, wa_ref, wb_ref, o_ref, ya_scr):
    for hd in range(NH_A):
        sl = slice(hd * DV_A, (hd + 1) * DV_A)
        h = hf_ref[:, sl] + hb_ref[:, sl]
        hc = h - jnp.mean(h, -1, keepdims=True)
        var = jnp.mean(hc * hc, -1, keepdims=True)
        hn = hc * lax.rsqrt(var + LN_EPS) * hg_ref[:, sl]
        ya_scr[:, sl] = (hn * jax.nn.sigmoid(oa_ref[:, sl])).astype(BF16)
    a = jnp.dot(ya_scr[...], wa_ref[...], preferred_element_type=F32)
    b = jnp.dot(yb_ref[...].astype(BF16), wb_ref[...], preferred_element_type=F32)
    o_ref[...] = (jax.nn.sigmoid(ga_ref[...]) * a + jax.nn.sigmoid(gb_ref[...]) * b).astype(BF16)


def _merge(h_f, h_b, proj, y_b, head_g, w_a, w_b):
    n = h_f.shape[0]
    tm = MERGE_TM
    const = lambda i: (0, 0)
    return pl.pallas_call(
        _merge_kernel,
        grid=(n // tm,),
        in_specs=[
            pl.BlockSpec((tm, D_A), lambda i: (i, 0)),
            pl.BlockSpec((tm, D_A), lambda i: (i, 0)),
            pl.BlockSpec((tm, D_A), lambda i: (i, OFF_OA // D_A)),
            pl.BlockSpec((tm, D_MODEL), lambda i: (i, OFF_GA // D_MODEL)),
            pl.BlockSpec((tm, D_MODEL), lambda i: (i, OFF_GB // D_MODEL)),
            pl.BlockSpec((tm, D_B), lambda i: (i, 0)),
            pl.BlockSpec((1, D_A), const),
            pl.BlockSpec((D_A, D_MODEL), const, pipeline_mode=pl.Buffered(1)),
            pl.BlockSpec((D_B, D_MODEL), const, pipeline_mode=pl.Buffered(1)),
        ],
        out_specs=pl.BlockSpec((tm, D_MODEL), lambda i: (i, 0)),
        out_shape=jax.ShapeDtypeStruct((n, D_MODEL), BF16),
        scratch_shapes=[pltpu.VMEM((tm, D_A), BF16)],
        compiler_params=_cparams(("arbitrary",)),
        name="merge",
    )(h_f, h_b, proj, proj, proj, y_b, head_g, w_a, w_b)


def _outproj_router_kernel(m_ref, x_ref, lg_ref, lb_ref, wo_ref, g1_ref, b1_ref, wr_hi_ref, wr_lo_ref, br_ref,
                           h1_ref, route_ref):
    tm = m_ref.shape[0]
    sub = tm // OUT_SUBTILES
    d = functools.partial(jnp.dot, preferred_element_type=F32)
    mixes = [d(m_ref[s * sub:(s + 1) * sub, :], wo_ref[...]) for s in range(OUT_SUBTILES)]
    parts = []
    for s in range(OUT_SUBTILES):
        rows = slice(s * sub, (s + 1) * sub)
        h0 = _layer_norm(x_ref[rows, :], lg_ref[...], lb_ref[...])
        h1 = _layer_norm(ALPHA * h0 + mixes[s], g1_ref[...], b1_ref[...])
        h1_ref[rows, :] = h1
        hi = h1.astype(BF16)
        parts.append((hi, (h1 - hi.astype(F32)).astype(BF16)))
    logits = jnp.concatenate(
        [d(hi, wr_hi_ref[...]) + (d(lo, wr_hi_ref[...]) + d(hi, wr_lo_ref[...])) for hi, lo in parts],
        axis=0) + br_ref[...]

    lane = lax.broadcasted_iota(jnp.int32, logits.shape, 1)
    big = jnp.int32(LANES)
    is_g = lane < N_GROUPS
    gl = jnp.where(is_g, logits, NEG_INF)
    gmax = jnp.max(gl, -1, keepdims=True)
    gsel = jnp.min(jnp.where(is_g & (gl == gmax), lane, big), -1, keepdims=True)
    pg = 1.0 / jnp.sum(jnp.where(is_g, jnp.exp(gl - gmax), 0.0), -1, keepdims=True)
    e_lo = N_GROUPS + gsel * EXPERTS_PER_GROUP
    in_grp = (lane >= e_lo) & (lane < e_lo + EXPERTS_PER_GROUP)
    el = jnp.where(in_grp, logits, NEG_INF)
    v0 = jnp.max(el, -1, keepdims=True)
    i0 = jnp.min(jnp.where(in_grp & (el == v0), lane, big), -1, keepdims=True)
    el1 = jnp.where(lane == i0, NEG_INF, el)
    v1 = jnp.max(el1, -1, keepdims=True)
    i1 = jnp.min(jnp.where(in_grp & (lane != i0) & (el1 == v1), lane, big), -1, keepdims=True)
    e1 = jnp.exp(v1 - v0)
    den = 1.0 + e1
    g0 = pg * (1.0 / den)
    g1 = pg * (e1 / den)
    out = jnp.where(lane == 0, (i0 - N_GROUPS).astype(F32),
                    jnp.where(lane == 1, (i1 - N_GROUPS).astype(F32),
                              jnp.where(lane == 2, g0, jnp.where(lane == 3, g1, 0.0))))
    route_ref[...] = out


def _outproj_router(merged, x, ln_in_g, ln_in_b, w_out, ln1_g, ln1_b, wr_hi, wr_lo, br):
    n = x.shape[0]
    tm = OUT_TM
    const = lambda i: (0, 0)
    row = lambda i: (i, 0)
    return pl.pallas_call(
        _outproj_router_kernel,
        grid=(n // tm,),
        in_specs=[
            pl.BlockSpec((tm, D_MODEL), row),
            pl.BlockSpec((tm, D_MODEL), row),
            pl.BlockSpec((1, D_MODEL), const),
            pl.BlockSpec((1, D_MODEL), const),
            pl.BlockSpec((D_MODEL, D_MODEL), const, pipeline_mode=pl.Buffered(1)),
            pl.BlockSpec((1, D_MODEL), const),
            pl.BlockSpec((1, D_MODEL), const),
            pl.BlockSpec((D_MODEL, LANES), const),
            pl.BlockSpec((D_MODEL, LANES), const),
            pl.BlockSpec((1, LANES), const),
        ],
        out_specs=[pl.BlockSpec((tm, D_MODEL), row), pl.BlockSpec((tm, LANES), row)],
        out_shape=[jax.ShapeDtypeStruct((n, D_MODEL), F32), jax.ShapeDtypeStruct((n, LANES), F32)],
        compiler_params=_cparams(("arbitrary",)),
        name="outproj_router",
    )(merged, x, ln_in_g, ln_in_b, w_out, ln1_g, ln1_b, wr_hi, wr_lo, br)


def _moe_ffn_kernel(be_ref, nu_ref, tok_ref, tokn_ref, h1_hbm, wg_ref, wu_ref, wd_ref, y_ref, xbuf, sem):
    i = pl.program_id(0)
    n_used = nu_ref[0]
    bm = xbuf.shape[1]
    slot = lax.rem(i, 2)
    nxt = 1 - slot

    def start_rows(t_ref, dst_slot, r0, r1):
        for r in range(r0, r1):
            pltpu.make_async_copy(h1_hbm.at[pl.ds(t_ref[0, 0, r], 1)], xbuf.at[dst_slot, pl.ds(r, 1)],
                                  sem.at[dst_slot]).start()

    def wait_rows(s):
        pltpu.make_async_copy(h1_hbm.at[pl.ds(0, bm)], xbuf.at[s], sem.at[s]).wait()

    @pl.when(i == 0)
    def _():
        start_rows(tok_ref, 0, 0, bm)

    @pl.when(i < n_used)
    def _():
        start_rows(tokn_ref, nxt, 0, bm)

    @pl.when(i < n_used)
    def _():
        wait_rows(slot)
        xb = xbuf[slot].astype(BF16)
        tn = (((0,), (1,)), ((), ()))
        tt = (((0,), (0,)), ((), ()))
        g_t = lax.dot_general(wg_ref[0], xb, tn, preferred_element_type=F32)
        u_t = lax.dot_general(wu_ref[0], xb, tn, preferred_element_type=F32)
        h_t = (jax.nn.silu(g_t) * u_t).astype(BF16)
        for half in range(2):
            sl = slice(half * D_EXPERT, (half + 1) * D_EXPERT)
            y_ref[:, sl] = lax.dot_general(wd_ref[0, :, sl], h_t, tt, preferred_element_type=F32).T

    @pl.when(i >= n_used)
    def _():
        @pl.when(i == n_used)
        def _():
            wait_rows(slot)

        y_ref[...] = jnp.zeros(y_ref.shape, F32)


def _moe_ffn(block_e, n_used, tok3, h1, w_gate, w_up, w_down):
    nb = tok3.shape[0]
    bm = MOE_BM
    wmap = lambda i, be, nu: (be[i], 0, 0)
    grid_spec = pltpu.PrefetchScalarGridSpec(
        num_scalar_prefetch=2,
        grid=(nb,),
        in_specs=[
            pl.BlockSpec((1, 1, bm), lambda i, be, nu: (i, 0, 0), memory_space=pltpu.SMEM),
            pl.BlockSpec((1, 1, bm), lambda i, be, nu: (jnp.minimum(i + 1, nb - 1), 0, 0),
                         memory_space=pltpu.SMEM),
            pl.BlockSpec(memory_space=pl.ANY),
            pl.BlockSpec((1, D_MODEL, D_EXPERT), wmap),
            pl.BlockSpec((1, D_MODEL, D_EXPERT), wmap),
            pl.BlockSpec((1, D_EXPERT, D_MODEL), wmap),
        ],
        out_specs=pl.BlockSpec((bm, D_MODEL), lambda i, be, nu: (i, 0)),
        scratch_shapes=[pltpu.VMEM((2, bm, D_MODEL), F32), pltpu.SemaphoreType.DMA((2,))],
    )
    return pl.pallas_call(
        _moe_ffn_kernel,
        grid_spec=grid_spec,
        out_shape=jax.ShapeDtypeStruct((nb * bm, D_MODEL), F32),
        compiler_params=_cparams(("arbitrary",)),
        name="moe_ffn",
    )(block_e, n_used, tok3, tok3, h1, w_gate, w_up, w_down)


def _combine_kernel(pos_ref, posn_ref, y_hbm, h1_ref, route_ref, g_ref, b_ref, o_ref, buf, sem):
    i = pl.program_id(0)
    tm = buf.shape[2]
    slot = lax.rem(i, 2)

    def start_rows(p_ref, s):
        for r in range(tm):
            for k in range(2):
                pltpu.make_async_copy(y_hbm.at[pl.ds(p_ref[0, 0, 2 * r + k], 1)],
                                      buf.at[s, k, pl.ds(r, 1)], sem.at[s]).start()

    @pl.when(i == 0)
    def _():
        start_rows(pos_ref, 0)

    @pl.when(i + 1 < pl.num_programs(0))
    def _():
        start_rows(posn_ref, 1 - slot)

    for k in range(2):
        pltpu.make_async_copy(y_hbm.at[pl.ds(0, tm)], buf.at[slot, k], sem.at[slot]).wait()
    route = route_ref[...]
    ffn = buf[slot, 0] * route[:, 2:3] + buf[slot, 1] * route[:, 3:4]
    o_ref[...] = _layer_norm(ALPHA * h1_ref[...] + ffn, g_ref[...], b_ref[...])


def _combine(pos3, y_sorted, h1, route, ln2_g, ln2_b):
    n = h1.shape[0]
    tm = COMB_TM
    nt = n // tm
    const = lambda i: (0, 0)
    row = lambda i: (i, 0)
    return pl.pallas_call(
        _combine_kernel,
        grid=(nt,),
        in_specs=[
            pl.BlockSpec((1, 1, 2 * tm), lambda i: (i, 0, 0), memory_space=pltpu.SMEM),
            pl.BlockSpec((1, 1, 2 * tm), lambda i: (jnp.minimum(i + 1, nt - 1), 0, 0), memory_space=pltpu.SMEM),
            pl.BlockSpec(memory_space=pl.ANY),
            pl.BlockSpec((tm, D_MODEL), row),
            pl.BlockSpec((tm, LANES), row),
            pl.BlockSpec((1, D_MODEL), const),
            pl.BlockSpec((1, D_MODEL), const),
        ],
        out_specs=pl.BlockSpec((tm, D_MODEL), row),
        out_shape=jax.ShapeDtypeStruct((n, D_MODEL), F32),
        scratch_shapes=[pltpu.VMEM((2, 2, tm, D_MODEL), F32), pltpu.SemaphoreType.DMA((2,))],
        compiler_params=_cparams(("arbitrary",)),
        name="combine",
    )(pos3, pos3, y_sorted, h1, route, ln2_g, ln2_b)


def _rank_kernel(route_ref, pstart_ref, pos_ref, carry):
    @pl.when(pl.program_id(0) == 0)
    def _():
        carry[...] = jnp.zeros(carry.shape, F32)

    tm = route_ref.shape[0]
    route = route_ref[...]
    lane = lax.broadcasted_iota(jnp.int32, (tm, LANES), 1)
    oh0 = lane == route[:, 0:1].astype(jnp.int32)
    oh1 = lane == route[:, 1:2].astype(jnp.int32)
    cnt = jnp.where(oh0, 1.0, 0.0) + jnp.where(oh1, 1.0, 0.0)
    r_i = lax.broadcasted_iota(jnp.int32, (tm, tm), 0)
    c_i = lax.broadcasted_iota(jnp.int32, (tm, tm), 1)
    earlier = jnp.dot((c_i < r_i).astype(BF16), cnt.astype(BF16), preferred_element_type=F32)
    row = earlier + (carry[...] + pstart_ref[...])
    pos0 = jnp.sum(jnp.where(oh0, row, 0.0), -1, keepdims=True)
    pos1 = jnp.sum(jnp.where(oh1, row, 0.0), -1, keepdims=True)
    pos_ref[...] = jnp.where(lane == 0, pos0, jnp.where(lane == 1, pos1, 0.0)).astype(jnp.int32)
    carry[...] += jnp.sum(cnt, axis=0, keepdims=True)


def _rank(route, pstart_row):
    n = route.shape[0]
    tm = RANK_TM
    return pl.pallas_call(
        _rank_kernel,
        grid=(n // tm,),
        in_specs=[pl.BlockSpec((tm, LANES), lambda i: (i, 0)), pl.BlockSpec((1, LANES), lambda i: (0, 0))],
        out_specs=pl.BlockSpec((tm, LANES), lambda i: (i, 0)),
        out_shape=jax.ShapeDtypeStruct((n, LANES), jnp.int32),
        scratch_shapes=[pltpu.VMEM((1, LANES), F32)],
        compiler_params=_cparams(("arbitrary",)),
        name="rank",
    )(route, pstart_row)


def _dispatch_plan(route, n):
    bm = MOE_BM
    a = 2 * n
    nb = a // bm + N_EXPERTS
    id_bits = max(a - 1, 1).bit_length()
    flat_e = route[:, 0:2].astype(jnp.int32).reshape(a)
    ids = jnp.arange(a, dtype=jnp.int32)
    order = jnp.sort(flat_e * (1 << id_bits) + ids) & ((1 << id_bits) - 1)
    experts = jnp.arange(N_EXPERTS, dtype=jnp.int32)
    counts = jnp.sum((flat_e[:, None] == experts[None, :]).astype(jnp.int32), axis=0)
    pcounts = (counts + bm - 1) // bm * bm
    incl = (experts[:, None] <= experts[None, :]).astype(jnp.int32)
    pend = pcounts @ incl
    pstart = pend - pcounts
    cstart = counts @ incl - counts
    blk0 = jnp.arange(nb, dtype=jnp.int32) * bm
    block_e = jnp.minimum(jnp.sum((pend[None, :] <= blk0[:, None]).astype(jnp.int32), axis=1),
                          N_EXPERTS - 1)
    off = (blk0 - pstart[block_e])[:, None] + jnp.arange(bm, dtype=jnp.int32)[None, :]
    valid = off < counts[block_e][:, None]
    src = jnp.clip(cstart[block_e][:, None] + off, 0, a - 1)
    tok = jnp.where(valid, order[src] >> 1, 0)
    n_used = (pend[-1:] // bm).astype(jnp.int32)
    pstart_row = jnp.pad(pstart.astype(F32), (0, LANES - N_EXPERTS)).reshape(1, LANES)
    pos = _rank(route, pstart_row)[:, :2]
    return (block_e.astype(jnp.int32), n_used, tok.reshape(nb, 1, bm),
            pos.reshape(n // COMB_TM, 1, 2 * COMB_TM))


def _encode(x, p):
    bsz, t_len, _ = x.shape
    n = bsz * t_len
    x2 = x.reshape(n, D_MODEL)
    proj, gates = _inproj(x2, p["ln_in_g"], p["ln_in_b"], p["w_main"], p["w_gates"])
    proj3 = proj.reshape(bsz, t_len, P_MAIN)
    gates3 = gates.reshape(bsz, t_len, LANES)
    gates_row = gates3[:, :, :4 * NH_A].transpose(0, 2, 1)
    qk, k_meta = _conv(proj3, p["proj_meta"], p["conv_w"], p["conv_b"])
    h_f, h_b = _mlstm(qk, proj3, gates3, gates_row, k_meta, p["proj_meta"], p["gates_meta"],
                      p["bias_col"], p["bias_row"])
    y_b = _natten(proj3, p["proj_meta"], p["bias_tab"])
    merged = _merge(h_f.reshape(n, D_A), h_b.reshape(n, D_A), proj, y_b.reshape(n, D_B),
                    p["head_g"], p["w_a"], p["w_b"])
    h1, route = _outproj_router(merged, x2, p["ln_in_g"], p["ln_in_b"], p["w_out"], p["ln1_g"], p["ln1_b"],
                                p["wr_hi"], p["wr_lo"], p["br"])
    block_e, n_used, tok3, pos3 = _dispatch_plan(route, n)
    y_sorted = _moe_ffn(block_e, n_used, tok3, h1, p["w_gate"], p["w_up"], p["w_down"])
    out = _combine(pos3, y_sorted, h1, route, p["ln2_g"], p["ln2_b"])
    return out.reshape(bsz, t_len, D_MODEL)


def kernel(x_prompt, x_sample, meta_tokens, ln_in_g, ln_in_b, w_in, b_gates, conv_w, conv_b, head_g, rpb, w_a, w_b, w_out, ln1_g, ln1_b, router_g_w, router_g_b, router_e_w, router_e_b, w_gate, w_up, w_down, ln2_g, ln2_b):
    l = 0
    offs = np.cumsum((0,) + IN_SPLITS)
    col = lambda i: w_in[l][:, offs[i]:offs[i + 1]]
    w_main = jnp.concatenate([col(0), col(1), col(2), col(3), col(8), col(9), col(5), col(6), col(7)],
                             axis=1).astype(BF16)
    w_gates = jnp.pad(col(4), ((0, 0), (0, LANES - 4 * NH_A))).astype(BF16)
    row = lambda v: v.reshape(1, -1).astype(F32)
    wr = jnp.pad(jnp.concatenate([router_g_w[l], router_e_w[l]], axis=1),
                 ((0, 0), (0, LANES - N_GROUPS - N_EXPERTS)))
    wr_hi = wr.astype(BF16)
    p = {
        "ln_in_g": row(ln_in_g), "ln_in_b": row(ln_in_b),
        "w_main": w_main, "w_gates": w_gates,
        "conv_w": conv_w[l], "conv_b": row(conv_b[l]),
        "bias_col": jnp.pad(b_gates[l], (0, LANES - 4 * NH_A)).reshape(1, LANES),
        "bias_row": b_gates[l].reshape(4 * NH_A, 1),
        "head_g": row(head_g[l]),
        "bias_tab": _natten_bias_table(rpb[l]),
        "w_a": w_a[l].astype(BF16), "w_b": w_b[l].astype(BF16), "w_out": w_out[l].astype(BF16),
        "ln1_g": row(ln1_g[l]), "ln1_b": row(ln1_b[l]),
        "wr_hi": wr_hi, "wr_lo": (wr - wr_hi.astype(F32)).astype(BF16),
        "br": jnp.pad(jnp.concatenate([router_g_b[l], router_e_b[l]]),
                      (0, LANES - N_GROUPS - N_EXPERTS)).reshape(1, LANES),
        "w_gate": w_gate[l].astype(BF16), "w_up": w_up[l].astype(BF16), "w_down": w_down[l].astype(BF16),
        "ln2_g": row(ln2_g[l]), "ln2_b": row(ln2_b[l]),
    }
    p["proj_meta"], p["gates_meta"] = _inproj(meta_tokens, p["ln_in_g"], p["ln_in_b"], w_main, w_gates)
    return (_encode(x_prompt, p), _encode(x_sample, p))
```

```python
import functools

import jax
import jax.numpy as jnp
import numpy as np
from jax import lax
from jax.experimental import pallas as pl
from jax.experimental.pallas import tpu as pltpu

F32 = jnp.float32
BF16 = jnp.bfloat16

D_MODEL = 2048
N_META = 16
GRID_W = 64
NH_A = 8
DK_A = 128
DV_A = 256
QK_A = NH_A * DK_A
D_A = NH_A * DV_A
NH_B = 16
DH_B = 64
D_B = NH_B * DH_B
KR_MAX = 8
KC = 16
Q_COL_BLOCK = 16
K_COL_SPAN = 32
N_GROUPS = 4
EXPERTS_PER_GROUP = 8
N_EXPERTS = N_GROUPS * EXPERTS_PER_GROUP
D_EXPERT = 1024
DEPTH = 1
ALPHA = (2 * DEPTH) ** 0.25
LN_EPS = 1e-5
NEG_INF = -1e30
IN_SPLITS = (QK_A, QK_A, D_A, D_A, 4 * NH_A, D_B, D_B, D_B, D_MODEL, D_MODEL)

OFF_QK = 0
OFF_QB = 2048
OFF_KB = 3072
OFF_VB = 4096
P_F32 = 5120
OFF_VA = 0
OFF_OA = 2048
OFF_GA = 4096
OFF_GB = 6144
P_BF16 = 8192
P_MAIN = P_F32 + P_BF16
LANES = 128

VMEM_LIMIT = 56 * 1024 * 1024

INPROJ_TM = 1024
INPROJ_TN = 1024
INPROJ_LN_SLAB = 256
CONV_TC = 512
MLSTM_CHUNK = 256
NAT_ROWS = 8
NAT_KROWS = 16
MERGE_TM = 256
OUT_TM = 512
OUT_SUBTILES = 2
MOE_BM = 256
COMB_TM = 256
RANK_TM = 512


def _cparams(sem):
    return pltpu.CompilerParams(dimension_semantics=sem, vmem_limit_bytes=VMEM_LIMIT)


def _layer_norm(x, g, b):
    xc = x - jnp.mean(x, -1, keepdims=True)
    var = jnp.mean(xc * xc, -1, keepdims=True)
    return xc * lax.rsqrt(var + LN_EPS) * g + b


def _inproj_kernel(x_ref, g_ref, b_ref, w_ref, wg_ref, of_ref, oh_ref, og_ref, h_scr):
    @pl.when(pl.program_id(1) == 0)
    def _():
        tm = x_ref.shape[0]
        slab = min(INPROJ_LN_SLAB, tm)

        def ln_slab(s, carry):
            r = pl.multiple_of(s * slab, slab)
            h = _layer_norm(x_ref[pl.ds(r, slab), :], g_ref[...], b_ref[...]).astype(BF16)
            h_scr[pl.ds(r, slab), :] = h
            og_ref[pl.ds(r, slab), :] = jnp.dot(h, wg_ref[...], preferred_element_type=F32)
            return carry

        lax.fori_loop(0, tm // slab, ln_slab, 0)

    n_f32 = P_F32 // w_ref.shape[1]

    @pl.when(pl.program_id(1) < n_f32)
    def _():
        of_ref[...] = jnp.dot(h_scr[...], w_ref[...], preferred_element_type=F32)

    @pl.when(pl.program_id(1) >= n_f32)
    def _():
        oh_ref[...] = jnp.dot(h_scr[...], w_ref[...], preferred_element_type=F32).astype(BF16)


def _inproj(x, ln_g, ln_b, w_main, w_gates):
    n = x.shape[0]
    tm = min(INPROJ_TM, n)
    tn = INPROJ_TN
    n_f32 = P_F32 // tn
    return pl.pallas_call(
        _inproj_kernel,
        grid=(n // tm, P_MAIN // tn),
        in_specs=[
            pl.BlockSpec((tm, D_MODEL), lambda i, j: (i, 0)),
            pl.BlockSpec((1, D_MODEL), lambda i, j: (0, 0)),
            pl.BlockSpec((1, D_MODEL), lambda i, j: (0, 0)),
            pl.BlockSpec((D_MODEL, tn), lambda i, j: (0, j)),
            pl.BlockSpec((D_MODEL, LANES), lambda i, j: (0, 0)),
        ],
        out_specs=[
            pl.BlockSpec((tm, tn), lambda i, j: (i, jnp.minimum(j, n_f32 - 1))),
            pl.BlockSpec((tm, tn), lambda i, j: (i, jnp.maximum(j - n_f32, 0))),
            pl.BlockSpec((tm, LANES), lambda i, j: (i, 0)),
        ],
        out_shape=[
            jax.ShapeDtypeStruct((n, P_F32), F32),
            jax.ShapeDtypeStruct((n, P_BF16), BF16),
            jax.ShapeDtypeStruct((n, LANES), F32),
        ],
        scratch_shapes=[pltpu.VMEM((tm, D_MODEL), BF16)],
        compiler_params=_cparams(("arbitrary", "arbitrary")),
        name="inproj",
    )(x, ln_g, ln_b, w_main, w_gates)


def _conv_kernel(x_ref, prev_ref, next_ref, meta_ref, w_ref, b_ref, qk_ref, kmeta_ref):
    t = pl.program_id(1)
    last = pl.num_programs(1) - 1
    tc = x_ref.shape[1]
    w0, w1, w2 = w_ref[0:1, :], w_ref[1:2, :], w_ref[2:3, :]
    bias = b_ref[...]
    lane = lax.broadcasted_iota(jnp.int32, (1, 2 * QK_A), 1)
    out_scale = jnp.where(lane >= QK_A, DK_A ** -0.5, 1.0).astype(F32)

    x = x_ref[0]
    meta = meta_ref[...]
    prev_row = jnp.where(t == 0, meta[N_META - 1:N_META, :], prev_ref[0, 7:8, :])
    next_row = jnp.where(t == last, jnp.zeros_like(prev_row), next_ref[0, 0:1, :])
    row = lax.broadcasted_iota(jnp.int32, (tc, 1), 0)
    x_prev = jnp.where(row == 0, prev_row, pltpu.roll(x, 1, 0))
    x_next = jnp.where(row == tc - 1, next_row, pltpu.roll(x, tc - 1, 0))
    y = x_prev * w0 + x * w1 + x_next * w2 + bias
    qk_ref[0] = (jax.nn.silu(y) * out_scale).astype(BF16)

    @pl.when(t == 0)
    def _():
        mrow = lax.broadcasted_iota(jnp.int32, (N_META, 1), 0)
        m_prev = jnp.where(mrow == 0, 0.0, pltpu.roll(meta, 1, 0))
        m_next = jnp.where(mrow == N_META - 1, x[0:1, :], pltpu.roll(meta, N_META - 1, 0))
        ym = m_prev * w0 + meta * w1 + m_next * w2 + bias
        km = jax.nn.silu(ym) * out_scale
        kmeta_ref[0] = km[:, QK_A:].astype(BF16)


def _conv(proj_f3, meta_f, conv_w, conv_b):
    bsz, t_len, _ = proj_f3.shape
    tc = CONV_TC
    nt = t_len // tc
    r8 = tc // 8
    return pl.pallas_call(
        _conv_kernel,
        grid=(bsz, nt),
        in_specs=[
            pl.BlockSpec((1, tc, 2 * QK_A), lambda b, t: (b, t, 0)),
            pl.BlockSpec((1, 8, 2 * QK_A), lambda b, t: (b, jnp.maximum(t * r8 - 1, 0), 0)),
            pl.BlockSpec((1, 8, 2 * QK_A), lambda b, t: (b, jnp.minimum((t + 1) * r8, t_len // 8 - 1), 0)),
            pl.BlockSpec((N_META, 2 * QK_A), lambda b, t: (0, 0)),
            pl.BlockSpec((3, 2 * QK_A), lambda b, t: (0, 0)),
            pl.BlockSpec((1, 2 * QK_A), lambda b, t: (0, 0)),
        ],
        out_specs=[
            pl.BlockSpec((1, tc, 2 * QK_A), lambda b, t: (b, t, 0)),
            pl.BlockSpec((1, N_META, QK_A), lambda b, t: (b, 0, 0)),
        ],
        out_shape=[
            jax.ShapeDtypeStruct((bsz, t_len, 2 * QK_A), BF16),
            jax.ShapeDtypeStruct((bsz, N_META, QK_A), BF16),
        ],
        compiler_params=_cparams(("arbitrary", "arbitrary")),
        name="conv_silu",
    )(proj_f3, proj_f3, proj_f3, meta_f, conv_w, conv_b)


def _split3(x):
    hi = x.astype(BF16)
    r1 = x - hi.astype(F32)
    mid = r1.astype(BF16)
    lo = (r1 - mid.astype(F32)).astype(BF16)
    return hi, mid, lo


def _tri_left(tri, x):
    hi, mid, lo = _split3(x)
    d = functools.partial(jnp.dot, preferred_element_type=F32)
    return d(tri, hi) + d(tri, mid) + d(tri, lo)


def _tri_right(x, tri):
    hi, mid, lo = _split3(x)
    d = functools.partial(jnp.dot, preferred_element_type=F32)
    return d(hi, tri) + d(mid, tri) + d(lo, tri)


def _state_update(k, v, ig_col, b_col, b_last, c_old, n_old, m_old):
    logw = b_last - b_col + ig_col
    m_new = jnp.maximum(b_last + m_old, jnp.max(logw, axis=0, keepdims=True))
    w = jnp.exp(logw - m_new)
    decay = jnp.exp(b_last + m_old - m_new)
    kw = k.astype(F32) * w
    kv = lax.dot_general(kw.astype(BF16), v, (((0,), (0,)), ((), ())), preferred_element_type=F32)
    c_new = decay * c_old + kv
    n_new = decay * n_old + jnp.sum(kw, axis=0, keepdims=True)
    return c_new, n_new, m_new


def _mlstm_kernel(qkf_ref, qkb_ref, vf_ref, vb_ref, gcf_ref, gcb_ref, grf_ref, grb_ref,
                  kmeta_ref, vmeta_ref, gmeta_ref, bcol_ref, brow_ref,
                  hf_ref, hb_ref, c_scr, n_scr, m_scr):
    j = pl.program_id(1)
    c = qkf_ref.shape[1]
    r_i = lax.broadcasted_iota(jnp.int32, (c, c), 0)
    c_i = lax.broadcasted_iota(jnp.int32, (c, c), 1)
    lower = r_i >= c_i
    upper = r_i <= c_i
    tri_l = lower.astype(BF16)
    tri_u = upper.astype(BF16)

    @pl.when(j == 0)
    def _():
        gm = gmeta_ref[...] + bcol_ref[...]
        lfm = jax.nn.log_sigmoid(gm)
        mr = lax.broadcasted_iota(jnp.int32, (N_META, N_META), 0)
        mc = lax.broadcasted_iota(jnp.int32, (N_META, N_META), 1)
        bm = _tri_left((mr >= mc).astype(BF16), lfm)
        zero_m = jnp.zeros((1, 1), F32)
        for hd in range(NH_A):
            ig_col = gm[:, hd:hd + 1]
            b_col = bm[:, 2 * NH_A + hd:2 * NH_A + hd + 1]
            b_last = b_col[N_META - 1:N_META, :]
            k = kmeta_ref[0, :, hd * DK_A:(hd + 1) * DK_A]
            v = vmeta_ref[:, hd * DV_A:(hd + 1) * DV_A].astype(BF16)
            c_new, n_new, m_new = _state_update(
                k, v, ig_col, b_col, b_last,
                jnp.zeros((DK_A, DV_A), F32), jnp.zeros((1, DK_A), F32), zero_m)
            c_scr[hd] = c_new
            n_scr[hd:hd + 1, :] = n_new
            m_scr[hd:hd + 1, :] = jnp.broadcast_to(m_new, (1, LANES))
        for hd in range(NH_A):
            u = NH_A + hd
            c_scr[u] = jnp.zeros((DK_A, DV_A), F32)
            n_scr[u:u + 1, :] = jnp.zeros((1, DK_A), F32)
            m_scr[u:u + 1, :] = jnp.zeros((1, LANES), F32)

    n_all = n_scr[...]
    m_all = m_scr[...]
    nt = (((1,), (1,)), ((), ()))
    units = []
    for direction in range(2):
        rev = direction == 1
        qk_ref = qkb_ref if rev else qkf_ref
        v_ref = vb_ref if rev else vf_ref
        for hd in range(NH_A):
            u = direction * NH_A + hd
            q = qk_ref[0, :, hd * DK_A:(hd + 1) * DK_A]
            k = qk_ref[0, :, QK_A + hd * DK_A:QK_A + (hd + 1) * DK_A]
            v = v_ref[0, :, hd * DV_A:(hd + 1) * DV_A].astype(BF16)
            c_old = c_scr[u]
            units.append(dict(
                rev=rev, hd=hd, u=u, q=q, k=k, v=v, c_old=c_old,
                n_old=n_all[u:u + 1, :], m_old=m_all[u:u + 1, 0:1],
                s_raw=lax.dot_general(q, k, nt, preferred_element_type=F32),
                q_c=jnp.dot(q, c_old.astype(BF16), preferred_element_type=F32)))

    for direction in range(2):
        rev = direction == 1
        gc = (gcb_ref if rev else gcf_ref)[0] + bcol_ref[...]
        gr = (grb_ref if rev else grf_ref)[0] + brow_ref[...]
        lf_c = jax.nn.log_sigmoid(gc)
        lf_r = jax.nn.log_sigmoid(gr)
        if rev:
            bc_all = _tri_left(tri_u, lf_c)
            br_all = _tri_right(lf_r, tri_l)
            mask = upper
        else:
            bc_all = _tri_left(tri_l, lf_c)
            br_all = _tri_right(lf_r, tri_u)
            mask = lower
        for hd in range(NH_A):
            un = units[direction * NH_A + hd]
            ig_lane = direction * NH_A + hd
            lf_lane = 2 * NH_A + direction * NH_A + hd
            ig_col = gc[:, ig_lane:ig_lane + 1]
            b_col = bc_all[:, lf_lane:lf_lane + 1]
            ig_row = gr[ig_lane:ig_lane + 1, :]
            b_row = br_all[lf_lane:lf_lane + 1, :]
            m_old = un["m_old"]
            logd = jnp.where(mask, b_col + (ig_row - b_row), NEG_INF)
            log_inter = b_col + m_old
            m_t = jnp.maximum(log_inter, jnp.max(logd, axis=-1, keepdims=True))
            s = un["s_raw"] * jnp.exp(logd - m_t)
            w_inter = jnp.exp(log_inter - m_t)
            q_n = jnp.sum(un["q"].astype(F32) * un["n_old"], axis=-1, keepdims=True)
            den = jnp.sum(s, axis=-1, keepdims=True) + w_inter * q_n
            inv = 1.0 / jnp.maximum(jnp.abs(den), jnp.exp(-m_t))
            b_last = b_col[0:1, :] if rev else b_col[c - 1:c, :]
            logw = b_last - b_col + ig_col
            m_new = jnp.maximum(b_last + m_old, jnp.max(logw, axis=0, keepdims=True))
            kw = un["k"].astype(F32) * jnp.exp(logw - m_new)
            un.update(s=s.astype(BF16), w_inter=w_inter, kw=kw, m_new=m_new, inv=inv,
                      decay=jnp.exp(b_last + m_old - m_new))

    tt = (((0,), (0,)), ((), ()))
    for un in units:
        un["sv"] = jnp.dot(un["s"], un["v"], preferred_element_type=F32)
        un["kv"] = lax.dot_general(un["kw"].astype(BF16), un["v"], tt, preferred_element_type=F32)

    n_out, m_out = [], []
    for un in units:
        out_ref = hb_ref if un["rev"] else hf_ref
        hd = un["hd"]
        h = (un["sv"] + un["w_inter"] * un["q_c"]) * un["inv"]
        out_ref[0, :, hd * DV_A:(hd + 1) * DV_A] = h.astype(out_ref.dtype)
        c_scr[un["u"]] = un["decay"] * un["c_old"] + un["kv"]
        n_out.append(un["decay"] * un["n_old"] + jnp.sum(un["kw"], axis=0, keepdims=True))
        m_out.append(jnp.broadcast_to(un["m_new"], (1, LANES)))
    n_scr[...] = jnp.concatenate(n_out, axis=0)
    m_scr[...] = jnp.concatenate(m_out, axis=0)


def _mlstm(qk, proj_h3, gates_col, gates_row, k_meta, meta_h, gates_meta, bias_col, bias_row):
    bsz, t_len, _ = qk.shape
    c = MLSTM_CHUNK
    nc = t_len // c
    va_blk = OFF_VA // D_A
    fwd = lambda b, j: (b, j, 0)
    bwd = lambda b, j: (b, nc - 1 - j, 0)
    return pl.pallas_call(
        _mlstm_kernel,
        grid=(bsz, nc),
        in_specs=[
            pl.BlockSpec((1, c, 2 * QK_A), fwd),
            pl.BlockSpec((1, c, 2 * QK_A), bwd),
            pl.BlockSpec((1, c, D_A), lambda b, j: (b, j, va_blk)),
            pl.BlockSpec((1, c, D_A), lambda b, j: (b, nc - 1 - j, va_blk)),
            pl.BlockSpec((1, c, LANES), fwd),
            pl.BlockSpec((1, c, LANES), bwd),
            pl.BlockSpec((1, 4 * NH_A, c), lambda b, j: (b, 0, j)),
            pl.BlockSpec((1, 4 * NH_A, c), lambda b, j: (b, 0, nc - 1 - j)),
            pl.BlockSpec((1, N_META, QK_A), lambda b, j: (b, 0, 0)),
            pl.BlockSpec((N_META, D_A), lambda b, j: (0, va_blk)),
            pl.BlockSpec((N_META, LANES), lambda b, j: (0, 0)),
            pl.BlockSpec((1, LANES), lambda b, j: (0, 0)),
            pl.BlockSpec((4 * NH_A, 1), lambda b, j: (0, 0)),
        ],
        out_specs=[
            pl.BlockSpec((1, c, D_A), fwd),
            pl.BlockSpec((1, c, D_A), bwd),
        ],
        out_shape=[
            jax.ShapeDtypeStruct((bsz, t_len, D_A), BF16),
            jax.ShapeDtypeStruct((bsz, t_len, D_A), BF16),
        ],
        scratch_shapes=[
            pltpu.VMEM((2 * NH_A, DK_A, DV_A), F32),
            pltpu.VMEM((2 * NH_A, DK_A), F32),
            pltpu.VMEM((2 * NH_A, LANES), F32),
        ],
        compiler_params=_cparams(("arbitrary", "arbitrary")),
        name="mlstm",
    )(qk, qk, proj_h3, proj_h3, gates_col, gates_col, gates_row, gates_row,
      k_meta, meta_h, gates_meta, bias_col, bias_row)


def _natten_col_blocks():
    cols = np.arange(GRID_W)
    cstart = np.clip(cols - KC // 2, 0, GRID_W - KC)
    ustart = np.clip(cstart[::Q_COL_BLOCK], 0, GRID_W - K_COL_SPAN)
    return cols, cstart, ustart


def _natten_bias_table(rpb):
    cols, cstart, ustart = _natten_col_blocks()
    ncb = GRID_W // Q_COL_BLOCK
    qcols = cols.reshape(ncb, Q_COL_BLOCK)[:, :, None]
    kcols = (ustart[:, None] + np.arange(K_COL_SPAN)[None, :])[:, None, :]
    qs = cstart.reshape(ncb, Q_COL_BLOCK)[:, :, None]
    col_valid = (kcols >= qs) & (kcols < qs + KC)
    col_idx = np.clip(kcols - qcols + KC - 1, 0, 2 * KC - 2)
    qr = np.arange(NAT_ROWS)[:, None]
    kk = np.arange(NAT_KROWS)[None, :]
    half = KR_MAX // 2
    rel_start = np.stack([np.maximum(qr - half, 0) + 0 * kk,
                          qr + 0 * kk,
                          np.minimum(qr + half, NAT_KROWS - KR_MAX) + 0 * kk])
    row_delta = np.stack([kk - qr, kk - qr - half, kk - qr - (NAT_KROWS - NAT_ROWS)])
    rpb_c = jnp.where(col_valid[None, None], rpb[:, :, col_idx], NEG_INF)
    rpb_c = rpb_c.transpose(0, 2, 3, 1, 4)
    kinds = []
    for kind in range(3):
        per_row = []
        for r in range(NAT_ROWS):
            first = int(rel_start[kind, r, 0])
            bias_row0 = int(row_delta[kind, r, first]) + KR_MAX - 1
            assert 0 <= bias_row0 and bias_row0 + KR_MAX <= 2 * KR_MAX - 1
            win = rpb_c[:, :, :, bias_row0:bias_row0 + KR_MAX, :]
            per_row.append(jnp.pad(win, ((0, 0), (0, 0), (0, 0), (first, NAT_KROWS - KR_MAX - first), (0, 0)),
                                   constant_values=NEG_INF))
        kinds.append(jnp.stack(per_row, axis=2))
    tab = jnp.stack(kinds, axis=0)
    return tab.reshape(3, NH_B, ncb, NAT_ROWS * Q_COL_BLOCK, NAT_KROWS * K_COL_SPAN)


def _natten_kernel(q_ref, k_ref, v_ref, km_ref, vm_ref, bias_ref, o_ref):
    i = pl.program_id(2)
    rows = k_ref.shape[1]
    ks = jnp.clip(i * NAT_ROWS - KR_MAX // 2, 0, rows - NAT_KROWS)
    _, _, ustart = _natten_col_blocks()
    nq = NAT_ROWS * Q_COL_BLOCK
    nk = NAT_KROWS * K_COL_SPAN
    lane = lax.broadcasted_iota(jnp.int32, (1, LANES), 1)
    kmeta = km_ref[...]
    vmeta = vm_ref[...].astype(BF16)
    nt = (((1,), (1,)), ((), ()))
    sels = [(lane >= h * DH_B) & (lane < (h + 1) * DH_B) for h in range(2)]
    kmh = [jnp.where(sel, kmeta, 0.0).astype(BF16) for sel in sels]
    ncb = GRID_W // Q_COL_BLOCK
    probs = []
    for j in range(ncb):
        u = int(ustart[j])
        qc = slice(j * Q_COL_BLOCK, (j + 1) * Q_COL_BLOCK)
        q = (q_ref[0, :, qc, :].reshape(nq, LANES) * (DH_B ** -0.5)).astype(BF16)
        kwin = k_ref[0, pl.ds(ks, NAT_KROWS), u:u + K_COL_SPAN, :].reshape(nk, LANES)
        vwin = v_ref[0, pl.ds(ks, NAT_KROWS), u:u + K_COL_SPAN, :].reshape(nk, LANES).astype(BF16)
        for h in range(2):
            kh = jnp.where(sels[h], kwin, 0.0).astype(BF16)
            probs.append(dict(
                j=j, h=h, vwin=vwin,
                s=lax.dot_general(q, kh, nt, preferred_element_type=F32),
                sm=lax.dot_general(q, kmh[h], nt, preferred_element_type=F32)))
    for pr in probs:
        s = pr["s"] + bias_ref[0, pr["h"], pr["j"]]
        sm = pr["sm"]
        mx = jnp.maximum(jnp.max(s, -1, keepdims=True), jnp.max(sm, -1, keepdims=True))
        p = jnp.exp(s - mx)
        pm = jnp.exp(sm - mx)
        pr.update(p=p.astype(BF16), pm=pm.astype(BF16),
                  inv=1.0 / (jnp.sum(p, -1, keepdims=True) + jnp.sum(pm, -1, keepdims=True)))
    for pr in probs:
        pr["o"] = (jnp.dot(pr["p"], pr["vwin"], preferred_element_type=F32)
                   + jnp.dot(pr["pm"], vmeta, preferred_element_type=F32)) * pr["inv"]
    for j in range(ncb):
        qc = slice(j * Q_COL_BLOCK, (j + 1) * Q_COL_BLOCK)
        out = jnp.where(sels[0], probs[2 * j]["o"], probs[2 * j + 1]["o"])
        o_ref[0, :, qc, :] = out.reshape(NAT_ROWS, Q_COL_BLOCK, LANES)


def _natten(proj_f3, meta_f, bias_tab):
    bsz, t_len, _ = proj_f3.shape
    rows = t_len // GRID_W
    assert rows % NAT_ROWS == 0 and rows >= NAT_KROWS
    nb = rows // NAT_ROWS
    hp = NH_B // 2
    ncb = GRID_W // Q_COL_BLOCK
    qb, kb, vb = OFF_QB // LANES, OFF_KB // LANES, OFF_VB // LANES
    proj4 = proj_f3.reshape(bsz, rows, GRID_W, P_F32)

    def pat(i):
        return jnp.where(i == 0, 0, jnp.where(i == nb - 1, 2, 1))

    out = pl.pallas_call(
        _natten_kernel,
        grid=(bsz, hp, nb),
        in_specs=[
            pl.BlockSpec((1, NAT_ROWS, GRID_W, LANES), lambda b, p, i: (b, i, 0, qb + p)),
            pl.BlockSpec((1, rows, GRID_W, LANES), lambda b, p, i: (b, 0, 0, kb + p)),
            pl.BlockSpec((1, rows, GRID_W, LANES), lambda b, p, i: (b, 0, 0, vb + p)),
            pl.BlockSpec((N_META, LANES), lambda b, p, i: (0, kb + p)),
            pl.BlockSpec((N_META, LANES), lambda b, p, i: (0, vb + p)),
            pl.BlockSpec((1, 2, ncb, NAT_ROWS * Q_COL_BLOCK, NAT_KROWS * K_COL_SPAN),
                         lambda b, p, i: (pat(i), p, 0, 0, 0)),
        ],
        out_specs=pl.BlockSpec((1, NAT_ROWS, GRID_W, LANES), lambda b, p, i: (b, i, 0, p)),
        out_shape=jax.ShapeDtypeStruct((bsz, rows, GRID_W, D_B), F32),
        compiler_params=_cparams(("arbitrary", "arbitrary", "arbitrary")),
        name="natten",
    )(proj4, proj4, proj4, meta_f, meta_f, bias_tab)
    return out.reshape(bsz, t_len, D_B)


def _merge_kernel(hf_ref, hb_ref, oa_ref, ga_ref, gb_ref, yb_ref, hg_ref, wa_ref, wb_ref, o_ref, ya_scr):
    for hd in range(NH_A):
        sl = slice(hd * DV_A, (hd + 1) * DV_A)
        h = hf_ref[:, sl].astype(F32) + hb_ref[:, sl].astype(F32)
        hc = h - jnp.mean(h, -1, keepdims=True)
        var = jnp.mean(hc * hc, -1, keepdims=True)
        hn = hc * lax.rsqrt(var + LN_EPS) * hg_ref[:, sl]
        ya_scr[:, sl] = (hn * jax.nn.sigmoid(oa_ref[:, sl].astype(F32))).astype(BF16)
    a = jnp.dot(ya_scr[...], wa_ref[...], preferred_element_type=F32)
    b = jnp.dot(yb_ref[...].astype(BF16), wb_ref[...], preferred_element_type=F32)
    gate_a = jax.nn.sigmoid(ga_ref[...].astype(F32))
    gate_b = jax.nn.sigmoid(gb_ref[...].astype(F32))
    o_ref[...] = (gate_a * a + gate_b * b).astype(BF16)


def _merge(h_f, h_b, proj_h, y_b, head_g, w_a, w_b):
    n = h_f.shape[0]
    tm = MERGE_TM
    const = lambda i: (0, 0)
    return pl.pallas_call(
        _merge_kernel,
        grid=(n // tm,),
        in_specs=[
            pl.BlockSpec((tm, D_A), lambda i: (i, 0)),
            pl.BlockSpec((tm, D_A), lambda i: (i, 0)),
            pl.BlockSpec((tm, D_A), lambda i: (i, OFF_OA // D_A)),
            pl.BlockSpec((tm, D_MODEL), lambda i: (i, OFF_GA // D_MODEL)),
            pl.BlockSpec((tm, D_MODEL), lambda i: (i, OFF_GB // D_MODEL)),
            pl.BlockSpec((tm, D_B), lambda i: (i, 0)),
            pl.BlockSpec((1, D_A), const),
            pl.BlockSpec((D_A, D_MODEL), const, pipeline_mode=pl.Buffered(1)),
            pl.BlockSpec((D_B, D_MODEL), const, pipeline_mode=pl.Buffered(1)),
        ],
        out_specs=pl.BlockSpec((tm, D_MODEL), lambda i: (i, 0)),
        out_shape=jax.ShapeDtypeStruct((n, D_MODEL), BF16),
        scratch_shapes=[pltpu.VMEM((tm, D_A), BF16)],
        compiler_params=_cparams(("arbitrary",)),
        name="merge",
    )(h_f, h_b, proj_h, proj_h, proj_h, y_b, head_g, w_a, w_b)


def _outproj_router_kernel(m_ref, x_ref, lg_ref, lb_ref, wo_ref, g1_ref, b1_ref, wr_hi_ref, wr_lo_ref, br_ref,
                           h1_ref, route_ref):
    tm = m_ref.shape[0]
    sub = tm // OUT_SUBTILES
    d = functools.partial(jnp.dot, preferred_element_type=F32)
    mixes = [d(m_ref[s * sub:(s + 1) * sub, :], wo_ref[...]) for s in range(OUT_SUBTILES)]
    parts = []
    for s in range(OUT_SUBTILES):
        rows = slice(s * sub, (s + 1) * sub)
        h0 = _layer_norm(x_ref[rows, :], lg_ref[...], lb_ref[...])
        h1 = _layer_norm(ALPHA * h0 + mixes[s], g1_ref[...], b1_ref[...])
        h1_ref[rows, :] = h1
        hi = h1.astype(BF16)
        parts.append((hi, (h1 - hi.astype(F32)).astype(BF16)))
    logits = jnp.concatenate(
        [d(hi, wr_hi_ref[...]) + (d(lo, wr_hi_ref[...]) + d(hi, wr_lo_ref[...])) for hi, lo in parts],
        axis=0) + br_ref[...]

    lane = lax.broadcasted_iota(jnp.int32, logits.shape, 1)
    big = jnp.int32(LANES)
    is_g = lane < N_GROUPS
    gl = jnp.where(is_g, logits, NEG_INF)
    gmax = jnp.max(gl, -1, keepdims=True)
    gsel = jnp.min(jnp.where(is_g & (gl == gmax), lane, big), -1, keepdims=True)
    pg = 1.0 / jnp.sum(jnp.where(is_g, jnp.exp(gl - gmax), 0.0), -1, keepdims=True)
    e_lo = N_GROUPS + gsel * EXPERTS_PER_GROUP
    in_grp = (lane >= e_lo) & (lane < e_lo + EXPERTS_PER_GROUP)
    el = jnp.where(in_grp, logits, NEG_INF)
    v0 = jnp.max(el, -1, keepdims=True)
    i0 = jnp.min(jnp.where(in_grp & (el == v0), lane, big), -1, keepdims=True)
    el1 = jnp.where(lane == i0, NEG_INF, el)
    v1 = jnp.max(el1, -1, keepdims=True)
    i1 = jnp.min(jnp.where(in_grp & (lane != i0) & (el1 == v1), lane, big), -1, keepdims=True)
    e1 = jnp.exp(v1 - v0)
    den = 1.0 + e1
    g0 = pg * (1.0 / den)
    g1 = pg * (e1 / den)
    out = jnp.where(lane == 0, (i0 - N_GROUPS).astype(F32),
                    jnp.where(lane == 1, (i1 - N_GROUPS).astype(F32),
                              jnp.where(lane == 2, g0, jnp.where(lane == 3, g1, 0.0))))
    route_ref[...] = out


def _outproj_router(merged, x, ln_in_g, ln_in_b, w_out, ln1_g, ln1_b, wr_hi, wr_lo, br):
    n = x.shape[0]
    tm = OUT_TM
    const = lambda i: (0, 0)
    row = lambda i: (i, 0)
    return pl.pallas_call(
        _outproj_router_kernel,
        grid=(n // tm,),
        in_specs=[
            pl.BlockSpec((tm, D_MODEL), row),
            pl.BlockSpec((tm, D_MODEL), row),
            pl.BlockSpec((1, D_MODEL), const),
            pl.BlockSpec((1, D_MODEL), const),
            pl.BlockSpec((D_MODEL, D_MODEL), const, pipeline_mode=pl.Buffered(1)),
            pl.BlockSpec((1, D_MODEL), const),
            pl.BlockSpec((1, D_MODEL), const),
            pl.BlockSpec((D_MODEL, LANES), const),
            pl.BlockSpec((D_MODEL, LANES), const),
            pl.BlockSpec((1, LANES), const),
        ],
        out_specs=[pl.BlockSpec((tm, D_MODEL), row), pl.BlockSpec((tm, LANES), row)],
        out_shape=[jax.ShapeDtypeStruct((n, D_MODEL), F32), jax.ShapeDtypeStruct((n, LANES), F32)],
        compiler_params=_cparams(("arbitrary",)),
        name="outproj_router",
    )(merged, x, ln_in_g, ln_in_b, w_out, ln1_g, ln1_b, wr_hi, wr_lo, br)


def _moe_ffn_kernel(be_ref, nu_ref, tok_ref, tokn_ref, h1_hbm, wg_ref, wu_ref, wd_ref, y_ref, xbuf, sem):
    i = pl.program_id(0)
    n_used = nu_ref[0]
    bm = xbuf.shape[1]
    slot = lax.rem(i, 2)
    nxt = 1 - slot

    def start_rows(t_ref, dst_slot, r0, r1):
        for r in range(r0, r1):
            pltpu.make_async_copy(h1_hbm.at[pl.ds(t_ref[0, 0, r], 1)], xbuf.at[dst_slot, pl.ds(r, 1)],
                                  sem.at[dst_slot]).start()

    def wait_rows(s):
        pltpu.make_async_copy(h1_hbm.at[pl.ds(0, bm)], xbuf.at[s], sem.at[s]).wait()

    @pl.when(i == 0)
    def _():
        start_rows(tok_ref, 0, 0, bm)

    @pl.when(i < n_used)
    def _():
        start_rows(tokn_ref, nxt, 0, bm)

    @pl.when(i < n_used)
    def _():
        wait_rows(slot)
        xb = xbuf[slot].astype(BF16)
        tn = (((0,), (1,)), ((), ()))
        tt = (((0,), (0,)), ((), ()))
        g_t = lax.dot_general(wg_ref[0], xb, tn, preferred_element_type=F32)
        u_t = lax.dot_general(wu_ref[0], xb, tn, preferred_element_type=F32)
        h_t = (jax.nn.silu(g_t) * u_t).astype(BF16)
        for half in range(2):
            sl = slice(half * D_EXPERT, (half + 1) * D_EXPERT)
            y_ref[:, sl] = lax.dot_general(wd_ref[0, :, sl], h_t, tt, preferred_element_type=F32).T

    @pl.when(i >= n_used)
    def _():
        @pl.when(i == n_used)
        def _():
            wait_rows(slot)

        y_ref[...] = jnp.zeros(y_ref.shape, F32)


def _moe_ffn(block_e, n_used, tok3, h1, w_gate, w_up, w_down):
    nb = tok3.shape[0]
    bm = MOE_BM
    wmap = lambda i, be, nu: (be[i], 0, 0)
    grid_spec = pltpu.PrefetchScalarGridSpec(
        num_scalar_prefetch=2,
        grid=(nb,),
        in_specs=[
            pl.BlockSpec((1, 1, bm), lambda i, be, nu: (i, 0, 0), memory_space=pltpu.SMEM),
            pl.BlockSpec((1, 1, bm), lambda i, be, nu: (jnp.minimum(i + 1, nb - 1), 0, 0),
                         memory_space=pltpu.SMEM),
            pl.BlockSpec(memory_space=pl.ANY),
            pl.BlockSpec((1, D_MODEL, D_EXPERT), wmap),
            pl.BlockSpec((1, D_MODEL, D_EXPERT), wmap),
            pl.BlockSpec((1, D_EXPERT, D_MODEL), wmap),
        ],
        out_specs=pl.BlockSpec((bm, D_MODEL), lambda i, be, nu: (i, 0)),
        scratch_shapes=[pltpu.VMEM((2, bm, D_MODEL), F32), pltpu.SemaphoreType.DMA((2,))],
    )
    return pl.pallas_call(
        _moe_ffn_kernel,
        grid_spec=grid_spec,
        out_shape=jax.ShapeDtypeStruct((nb * bm, D_MODEL), F32),
        compiler_params=_cparams(("arbitrary",)),
        name="moe_ffn",
    )(block_e, n_used, tok3, tok3, h1, w_gate, w_up, w_down)


def _combine_kernel(pos_ref, posn_ref, y_hbm, h1_ref, route_ref, g_ref, b_ref, o_ref, buf, sem):
    i = pl.program_id(0)
    tm = buf.shape[2]
    slot = lax.rem(i, 2)

    def start_rows(p_ref, s):
        for r in range(tm):
            for k in range(2):
                pltpu.make_async_copy(y_hbm.at[pl.ds(p_ref[0, 0, 2 * r + k], 1)],
                                      buf.at[s, k, pl.ds(r, 1)], sem.at[s]).start()

    @pl.when(i == 0)
    def _():
        start_rows(pos_ref, 0)

    @pl.when(i + 1 < pl.num_programs(0))
    def _():
        start_rows(posn_ref, 1 - slot)

    for k in range(2):
        pltpu.make_async_copy(y_hbm.at[pl.ds(0, tm)], buf.at[slot, k], sem.at[slot]).wait()
    route = route_ref[...]
    ffn = buf[slot, 0] * route[:, 2:3] + buf[slot, 1] * route[:, 3:4]
    o_ref[...] = _layer_norm(ALPHA * h1_ref[...] + ffn, g_ref[...], b_ref[...])


def _combine(pos3, y_sorted, h1, route, ln2_g, ln2_b):
    n = h1.shape[0]
    tm = COMB_TM
    nt = n // tm
    const = lambda i: (0, 0)
    row = lambda i: (i, 0)
    return pl.pallas_call(
        _combine_kernel,
        grid=(nt,),
        in_specs=[
            pl.BlockSpec((1, 1, 2 * tm), lambda i: (i, 0, 0), memory_space=pltpu.SMEM),
            pl.BlockSpec((1, 1, 2 * tm), lambda i: (jnp.minimum(i + 1, nt - 1), 0, 0), memory_space=pltpu.SMEM),
            pl.BlockSpec(memory_space=pl.ANY),
            pl.BlockSpec((tm, D_MODEL), row),
            pl.BlockSpec((tm, LANES), row),
            pl.BlockSpec((1, D_MODEL), const),
            pl.BlockSpec((1, D_MODEL), const),
        ],
        out_specs=pl.BlockSpec((tm, D_MODEL), row),
        out_shape=jax.ShapeDtypeStruct((n, D_MODEL), F32),
        scratch_shapes=[pltpu.VMEM((2, 2, tm, D_MODEL), F32), pltpu.SemaphoreType.DMA((2,))],
        compiler_params=_cparams(("arbitrary",)),
        name="combine",
    )(pos3, pos3, y_sorted, h1, route, ln2_g, ln2_b)


def _rank_kernel(route_ref, pstart_ref, pos_ref, carry):
    @pl.when(pl.program_id(0) == 0)
    def _():
        carry[...] = jnp.zeros(carry.shape, F32)

    tm = route_ref.shape[0]
    route = route_ref[...]
    lane = lax.broadcasted_iota(jnp.int32, (tm, LANES), 1)
    oh0 = lane == route[:, 0:1].astype(jnp.int32)
    oh1 = lane == route[:, 1:2].astype(jnp.int32)
    cnt = jnp.where(oh0, 1.0, 0.0) + jnp.where(oh1, 1.0, 0.0)
    r_i = lax.broadcasted_iota(jnp.int32, (tm, tm), 0)
    c_i = lax.broadcasted_iota(jnp.int32, (tm, tm), 1)
    earlier = jnp.dot((c_i < r_i).astype(BF16), cnt.astype(BF16), preferred_element_type=F32)
    row = earlier + (carry[...] + pstart_ref[...])
    pos0 = jnp.sum(jnp.where(oh0, row, 0.0), -1, keepdims=True)
    pos1 = jnp.sum(jnp.where(oh1, row, 0.0), -1, keepdims=True)
    pos_ref[...] = jnp.where(lane == 0, pos0, jnp.where(lane == 1, pos1, 0.0)).astype(jnp.int32)
    carry[...] += jnp.sum(cnt, axis=0, keepdims=True)


def _rank(route, pstart_row):
    n = route.shape[0]
    tm = RANK_TM
    return pl.pallas_call(
        _rank_kernel,
        grid=(n // tm,),
        in_specs=[pl.BlockSpec((tm, LANES), lambda i: (i, 0)), pl.BlockSpec((1, LANES), lambda i: (0, 0))],
        out_specs=pl.BlockSpec((tm, LANES), lambda i: (i, 0)),
        out_shape=jax.ShapeDtypeStruct((n, LANES), jnp.int32),
        scratch_shapes=[pltpu.VMEM((1, LANES), F32)],
        compiler_params=_cparams(("arbitrary",)),
        name="rank",
    )(route, pstart_row)


def _dispatch_plan(route, n):
    bm = MOE_BM
    a = 2 * n
    nb = a // bm + N_EXPERTS
    id_bits = max(a - 1, 1).bit_length()
    flat_e = route[:, 0:2].astype(jnp.int32).reshape(a)
    ids = jnp.arange(a, dtype=jnp.int32)
    order = jnp.sort(flat_e * (1 << id_bits) + ids) & ((1 << id_bits) - 1)
    experts = jnp.arange(N_EXPERTS, dtype=jnp.int32)
    counts = jnp.sum((flat_e[:, None] == experts[None, :]).astype(jnp.int32), axis=0)
    pcounts = (counts + bm - 1) // bm * bm
    incl = (experts[:, None] <= experts[None, :]).astype(jnp.int32)
    pend = pcounts @ incl
    pstart = pend - pcounts
    cstart = counts @ incl - counts
    blk0 = jnp.arange(nb, dtype=jnp.int32) * bm
    block_e = jnp.minimum(jnp.sum((pend[None, :] <= blk0[:, None]).astype(jnp.int32), axis=1),
                          N_EXPERTS - 1)
    off = (blk0 - pstart[block_e])[:, None] + jnp.arange(bm, dtype=jnp.int32)[None, :]
    valid = off < counts[block_e][:, None]
    src = jnp.clip(cstart[block_e][:, None] + off, 0, a - 1)
    tok = jnp.where(valid, order[src] >> 1, 0)
    n_used = (pend[-1:] // bm).astype(jnp.int32)
    pstart_row = jnp.pad(pstart.astype(F32), (0, LANES - N_EXPERTS)).reshape(1, LANES)
    pos = _rank(route, pstart_row)[:, :2]
    return (block_e.astype(jnp.int32), n_used, tok.reshape(nb, 1, bm),
            pos.reshape(n // COMB_TM, 1, 2 * COMB_TM))


def _encode(x, p):
    bsz, t_len, _ = x.shape
    n = bsz * t_len
    x2 = x.reshape(n, D_MODEL)
    proj_f, proj_h, gates = _inproj(x2, p["ln_in_g"], p["ln_in_b"], p["w_main"], p["w_gates"])
    proj_f3 = proj_f.reshape(bsz, t_len, P_F32)
    proj_h3 = proj_h.reshape(bsz, t_len, P_BF16)
    gates3 = gates.reshape(bsz, t_len, LANES)
    gates_row = gates3[:, :, :4 * NH_A].transpose(0, 2, 1)
    qk, k_meta = _conv(proj_f3, p["meta_f"], p["conv_w"], p["conv_b"])
    h_f, h_b = _mlstm(qk, proj_h3, gates3, gates_row, k_meta, p["meta_h"], p["gates_meta"],
                      p["bias_col"], p["bias_row"])
    y_b = _natten(proj_f3, p["meta_f"], p["bias_tab"])
    merged = _merge(h_f.reshape(n, D_A), h_b.reshape(n, D_A), proj_h, y_b.reshape(n, D_B),
                    p["head_g"], p["w_a"], p["w_b"])
    h1, route = _outproj_router(merged, x2, p["ln_in_g"], p["ln_in_b"], p["w_out"], p["ln1_g"], p["ln1_b"],
                                p["wr_hi"], p["wr_lo"], p["br"])
    block_e, n_used, tok3, pos3 = _dispatch_plan(route, n)
    y_sorted = _moe_ffn(block_e, n_used, tok3, h1, p["w_gate"], p["w_up"], p["w_down"])
    out = _combine(pos3, y_sorted, h1, route, p["ln2_g"], p["ln2_b"])
    return out.reshape(bsz, t_len, D_MODEL)


def kernel(x_prompt, x_sample, meta_tokens, ln_in_g, ln_in_b, w_in, b_gates, conv_w, conv_b, head_g, rpb, w_a, w_b, w_out, ln1_g, ln1_b, router_g_w, router_g_b, router_e_w, router_e_b, w_gate, w_up, w_down, ln2_g, ln2_b):
    l = 0
    offs = np.cumsum((0,) + IN_SPLITS)
    col = lambda i: w_in[l][:, offs[i]:offs[i + 1]]
    w_main = jnp.concatenate([col(0), col(1), col(5), col(6), col(7), col(2), col(3), col(8), col(9)],
                             axis=1).astype(BF16)
    w_gates = jnp.pad(col(4), ((0, 0), (0, LANES - 4 * NH_A))).astype(BF16)
    row = lambda v: v.reshape(1, -1).astype(F32)
    wr = jnp.pad(jnp.concatenate([router_g_w[l], router_e_w[l]], axis=1),
                 ((0, 0), (0, LANES - N_GROUPS - N_EXPERTS)))
    wr_hi = wr.astype(BF16)
    p = {
        "ln_in_g": row(ln_in_g), "ln_in_b": row(ln_in_b),
        "w_main": w_main, "w_gates": w_gates,
        "conv_w": conv_w[l], "conv_b": row(conv_b[l]),
        "bias_col": jnp.pad(b_gates[l], (0, LANES - 4 * NH_A)).reshape(1, LANES),
        "bias_row": b_gates[l].reshape(4 * NH_A, 1),
        "head_g": row(head_g[l]),
        "bias_tab": _natten_bias_table(rpb[l]),
        "w_a": w_a[l].astype(BF16), "w_b": w_b[l].astype(BF16), "w_out": w_out[l].astype(BF16),
        "ln1_g": row(ln1_g[l]), "ln1_b": row(ln1_b[l]),
        "wr_hi": wr_hi, "wr_lo": (wr - wr_hi.astype(F32)).astype(BF16),
        "br": jnp.pad(jnp.concatenate([router_g_b[l], router_e_b[l]]),
                      (0, LANES - N_GROUPS - N_EXPERTS)).reshape(1, LANES),
        "w_gate": w_gate[l].astype(BF16), "w_up": w_up[l].astype(BF16), "w_down": w_down[l].astype(BF16),
        "ln2_g": row(ln2_g[l]), "ln2_b": row(ln2_b[l]),
    }
    p["meta_f"], p["meta_h"], p["gates_meta"] = _inproj(meta_tokens, p["ln_in_g"], p["ln_in_b"], w_main, w_gates)
    return (_encode(x_prompt, p), _encode(x_sample, p))
```

```python
import functools

import jax
import jax.numpy as jnp
import numpy as np
from jax import lax
from jax.experimental import pallas as pl
from jax.experimental.pallas import tpu as pltpu

F32 = jnp.float32
BF16 = jnp.bfloat16

D_MODEL = 2048
N_META = 16
GRID_W = 64
NH_A = 8
DK_A = 128
DV_A = 256
QK_A = NH_A * DK_A
D_A = NH_A * DV_A
NH_B = 16
DH_B = 64
D_B = NH_B * DH_B
KR_MAX = 8
KC = 16
Q_COL_BLOCK = 16
K_COL_SPAN = 32
N_GROUPS = 4
EXPERTS_PER_GROUP = 8
N_EXPERTS = N_GROUPS * EXPERTS_PER_GROUP
D_EXPERT = 1024
DEPTH = 1
ALPHA = (2 * DEPTH) ** 0.25
LN_EPS = 1e-5
NEG_INF = -1e30
IN_SPLITS = (QK_A, QK_A, D_A, D_A, 4 * NH_A, D_B, D_B, D_B, D_MODEL, D_MODEL)

OFF_QK = 0
OFF_QB = 2048
OFF_KB = 3072
OFF_VB = 4096
P_F32 = 5120
OFF_VA = 0
OFF_OA = 2048
OFF_GA = 4096
OFF_GB = 6144
P_BF16 = 8192
P_MAIN = P_F32 + P_BF16
LANES = 128

VMEM_LIMIT = 56 * 1024 * 1024

INPROJ_TM = 1024
INPROJ_TN = 1024
INPROJ_LN_SLAB = 256
CONV_TC = 512
MLSTM_CHUNK = 256
NAT_ROWS = 8
NAT_KROWS = 16
MERGE_TM = 512
MERGE_SUBTILES = 2
OUT_TM = 512
OUT_SUBTILES = 2
MOE_BM = 256
COMB_TM = 256
RANK_TM = 512


def _cparams(sem):
    return pltpu.CompilerParams(dimension_semantics=sem, vmem_limit_bytes=VMEM_LIMIT)


def _layer_norm(x, g, b):
    xc = x - jnp.mean(x, -1, keepdims=True)
    var = jnp.mean(xc * xc, -1, keepdims=True)
    return xc * lax.rsqrt(var + LN_EPS) * g + b


def _inproj_kernel(x_ref, g_ref, b_ref, w_ref, wg_ref, of_ref, oh_ref, og_ref, h_scr):
    @pl.when(pl.program_id(1) == 0)
    def _():
        tm = x_ref.shape[0]
        slab = min(INPROJ_LN_SLAB, tm)

        def ln_slab(s, carry):
            r = pl.multiple_of(s * slab, slab)
            h = _layer_norm(x_ref[pl.ds(r, slab), :], g_ref[...], b_ref[...]).astype(BF16)
            h_scr[pl.ds(r, slab), :] = h
            og_ref[pl.ds(r, slab), :] = jnp.dot(h, wg_ref[...], preferred_element_type=F32)
            return carry

        lax.fori_loop(0, tm // slab, ln_slab, 0)

    n_f32 = P_F32 // w_ref.shape[1]

    @pl.when(pl.program_id(1) < n_f32)
    def _():
        of_ref[...] = jnp.dot(h_scr[...], w_ref[...], preferred_element_type=F32)

    @pl.when(pl.program_id(1) >= n_f32)
    def _():
        oh_ref[...] = jnp.dot(h_scr[...], w_ref[...], preferred_element_type=F32).astype(BF16)


def _inproj(x, ln_g, ln_b, w_main, w_gates):
    n = x.shape[0]
    tm = min(INPROJ_TM, n)
    tn = INPROJ_TN
    n_f32 = P_F32 // tn
    return pl.pallas_call(
        _inproj_kernel,
        grid=(n // tm, P_MAIN // tn),
        in_specs=[
            pl.BlockSpec((tm, D_MODEL), lambda i, j: (i, 0)),
            pl.BlockSpec((1, D_MODEL), lambda i, j: (0, 0)),
            pl.BlockSpec((1, D_MODEL), lambda i, j: (0, 0)),
            pl.BlockSpec((D_MODEL, tn), lambda i, j: (0, j)),
            pl.BlockSpec((D_MODEL, LANES), lambda i, j: (0, 0)),
        ],
        out_specs=[
            pl.BlockSpec((tm, tn), lambda i, j: (i, jnp.minimum(j, n_f32 - 1))),
            pl.BlockSpec((tm, tn), lambda i, j: (i, jnp.maximum(j - n_f32, 0))),
            pl.BlockSpec((tm, LANES), lambda i, j: (i, 0)),
        ],
        out_shape=[
            jax.ShapeDtypeStruct((n, P_F32), F32),
            jax.ShapeDtypeStruct((n, P_BF16), BF16),
            jax.ShapeDtypeStruct((n, LANES), F32),
        ],
        scratch_shapes=[pltpu.VMEM((tm, D_MODEL), BF16)],
        compiler_params=_cparams(("arbitrary", "arbitrary")),
        name="inproj",
    )(x, ln_g, ln_b, w_main, w_gates)


def _conv_kernel(x_ref, prev_ref, next_ref, meta_ref, w_ref, b_ref, qk_ref, kmeta_ref):
    t = pl.program_id(1)
    last = pl.num_programs(1) - 1
    tc = x_ref.shape[1]
    w0, w1, w2 = w_ref[0:1, :], w_ref[1:2, :], w_ref[2:3, :]
    bias = b_ref[...]
    lane = lax.broadcasted_iota(jnp.int32, (1, 2 * QK_A), 1)
    out_scale = jnp.where(lane >= QK_A, DK_A ** -0.5, 1.0).astype(F32)

    x = x_ref[0]
    meta = meta_ref[...]
    prev_row = jnp.where(t == 0, meta[N_META - 1:N_META, :], prev_ref[0, 7:8, :])
    next_row = jnp.where(t == last, jnp.zeros_like(prev_row), next_ref[0, 0:1, :])
    row = lax.broadcasted_iota(jnp.int32, (tc, 1), 0)
    x_prev = jnp.where(row == 0, prev_row, pltpu.roll(x, 1, 0))
    x_next = jnp.where(row == tc - 1, next_row, pltpu.roll(x, tc - 1, 0))
    y = x_prev * w0 + x * w1 + x_next * w2 + bias
    qk_ref[0] = (jax.nn.silu(y) * out_scale).astype(BF16)

    @pl.when(t == 0)
    def _():
        mrow = lax.broadcasted_iota(jnp.int32, (N_META, 1), 0)
        m_prev = jnp.where(mrow == 0, 0.0, pltpu.roll(meta, 1, 0))
        m_next = jnp.where(mrow == N_META - 1, x[0:1, :], pltpu.roll(meta, N_META - 1, 0))
        ym = m_prev * w0 + meta * w1 + m_next * w2 + bias
        km = jax.nn.silu(ym) * out_scale
        kmeta_ref[0] = km[:, QK_A:].astype(BF16)


def _conv(proj_f3, meta_f, conv_w, conv_b):
    bsz, t_len, _ = proj_f3.shape
    tc = CONV_TC
    nt = t_len // tc
    r8 = tc // 8
    return pl.pallas_call(
        _conv_kernel,
        grid=(bsz, nt),
        in_specs=[
            pl.BlockSpec((1, tc, 2 * QK_A), lambda b, t: (b, t, 0)),
            pl.BlockSpec((1, 8, 2 * QK_A), lambda b, t: (b, jnp.maximum(t * r8 - 1, 0), 0)),
            pl.BlockSpec((1, 8, 2 * QK_A), lambda b, t: (b, jnp.minimum((t + 1) * r8, t_len // 8 - 1), 0)),
            pl.BlockSpec((N_META, 2 * QK_A), lambda b, t: (0, 0)),
            pl.BlockSpec((3, 2 * QK_A), lambda b, t: (0, 0)),
            pl.BlockSpec((1, 2 * QK_A), lambda b, t: (0, 0)),
        ],
        out_specs=[
            pl.BlockSpec((1, tc, 2 * QK_A), lambda b, t: (b, t, 0)),
            pl.BlockSpec((1, N_META, QK_A), lambda b, t: (b, 0, 0)),
        ],
        out_shape=[
            jax.ShapeDtypeStruct((bsz, t_len, 2 * QK_A), BF16),
            jax.ShapeDtypeStruct((bsz, N_META, QK_A), BF16),
        ],
        compiler_params=_cparams(("arbitrary", "arbitrary")),
        name="conv_silu",
    )(proj_f3, proj_f3, proj_f3, meta_f, conv_w, conv_b)


def _split3(x):
    hi = x.astype(BF16)
    r1 = x - hi.astype(F32)
    mid = r1.astype(BF16)
    lo = (r1 - mid.astype(F32)).astype(BF16)
    return hi, mid, lo


def _tri_left(tri, x):
    hi, mid, lo = _split3(x)
    d = functools.partial(jnp.dot, preferred_element_type=F32)
    return d(tri, hi) + d(tri, mid) + d(tri, lo)


def _tri_right(x, tri):
    hi, mid, lo = _split3(x)
    d = functools.partial(jnp.dot, preferred_element_type=F32)
    return d(hi, tri) + d(mid, tri) + d(lo, tri)


def _state_update(k, v, ig_col, b_col, b_last, c_old, n_old, m_old):
    logw = b_last - b_col + ig_col
    m_new = jnp.maximum(b_last + m_old, jnp.max(logw, axis=0, keepdims=True))
    w = jnp.exp(logw - m_new)
    decay = jnp.exp(b_last + m_old - m_new)
    kw = k.astype(F32) * w
    kv = lax.dot_general(kw.astype(BF16), v, (((0,), (0,)), ((), ())), preferred_element_type=F32)
    c_new = decay * c_old + kv
    n_new = decay * n_old + jnp.sum(kw, axis=0, keepdims=True)
    return c_new, n_new, m_new


def _mlstm_kernel(qkf_ref, qkb_ref, vf_ref, vb_ref, gcf_ref, gcb_ref, grf_ref, grb_ref,
                  kmeta_ref, vmeta_ref, gmeta_ref, bcol_ref, brow_ref,
                  hf_ref, hb_ref, c_scr, n_scr, m_scr):
    j = pl.program_id(1)
    c = qkf_ref.shape[1]
    r_i = lax.broadcasted_iota(jnp.int32, (c, c), 0)
    c_i = lax.broadcasted_iota(jnp.int32, (c, c), 1)
    lower = r_i >= c_i
    upper = r_i <= c_i
    tri_l = lower.astype(BF16)
    tri_u = upper.astype(BF16)

    @pl.when(j == 0)
    def _():
        gm = gmeta_ref[...] + bcol_ref[...]
        lfm = jax.nn.log_sigmoid(gm)
        mr = lax.broadcasted_iota(jnp.int32, (N_META, N_META), 0)
        mc = lax.broadcasted_iota(jnp.int32, (N_META, N_META), 1)
        bm = _tri_left((mr >= mc).astype(BF16), lfm)
        zero_m = jnp.zeros((1, 1), F32)
        for hd in range(NH_A):
            ig_col = gm[:, hd:hd + 1]
            b_col = bm[:, 2 * NH_A + hd:2 * NH_A + hd + 1]
            b_last = b_col[N_META - 1:N_META, :]
            k = kmeta_ref[0, :, hd * DK_A:(hd + 1) * DK_A]
            v = vmeta_ref[:, hd * DV_A:(hd + 1) * DV_A].astype(BF16)
            c_new, n_new, m_new = _state_update(
                k, v, ig_col, b_col, b_last,
                jnp.zeros((DK_A, DV_A), F32), jnp.zeros((1, DK_A), F32), zero_m)
            c_scr[hd] = c_new
            n_scr[hd:hd + 1, :] = n_new
            m_scr[hd:hd + 1, :] = jnp.broadcast_to(m_new, (1, LANES))
        for hd in range(NH_A):
            u = NH_A + hd
            c_scr[u] = jnp.zeros((DK_A, DV_A), F32)
            n_scr[u:u + 1, :] = jnp.zeros((1, DK_A), F32)
            m_scr[u:u + 1, :] = jnp.zeros((1, LANES), F32)

    n_all = n_scr[...]
    m_all = m_scr[...]
    nt = (((1,), (1,)), ((), ()))
    units = []
    for direction in range(2):
        rev = direction == 1
        qk_ref = qkb_ref if rev else qkf_ref
        v_ref = vb_ref if rev else vf_ref
        for hd in range(NH_A):
            u = direction * NH_A + hd
            q = qk_ref[0, :, hd * DK_A:(hd + 1) * DK_A]
            k = qk_ref[0, :, QK_A + hd * DK_A:QK_A + (hd + 1) * DK_A]
            v = v_ref[0, :, hd * DV_A:(hd + 1) * DV_A].astype(BF16)
            c_old = c_scr[u]
            units.append(dict(
                rev=rev, hd=hd, u=u, q=q, k=k, v=v, c_old=c_old,
                n_old=n_all[u:u + 1, :], m_old=m_all[u:u + 1, 0:1],
                s_raw=lax.dot_general(q, k, nt, preferred_element_type=F32),
                q_c=jnp.dot(q, c_old.astype(BF16), preferred_element_type=F32)))

    for direction in range(2):
        rev = direction == 1
        gc = (gcb_ref if rev else gcf_ref)[0] + bcol_ref[...]
        gr = (grb_ref if rev else grf_ref)[0] + brow_ref[...]
        lf_c = jax.nn.log_sigmoid(gc)
        lf_r = jax.nn.log_sigmoid(gr)
        if rev:
            bc_all = _tri_left(tri_u, lf_c)
            br_all = _tri_right(lf_r, tri_l)
            mask = upper
        else:
            bc_all = _tri_left(tri_l, lf_c)
            br_all = _tri_right(lf_r, tri_u)
            mask = lower
        for hd in range(NH_A):
            un = units[direction * NH_A + hd]
            ig_lane = direction * NH_A + hd
            lf_lane = 2 * NH_A + direction * NH_A + hd
            ig_col = gc[:, ig_lane:ig_lane + 1]
            b_col = bc_all[:, lf_lane:lf_lane + 1]
            ig_row = gr[ig_lane:ig_lane + 1, :]
            b_row = br_all[lf_lane:lf_lane + 1, :]
            m_old = un["m_old"]
            logd = jnp.where(mask, b_col + (ig_row - b_row), NEG_INF)
            log_inter = b_col + m_old
            m_t = jnp.maximum(log_inter, jnp.max(logd, axis=-1, keepdims=True))
            s = un["s_raw"] * jnp.exp(logd - m_t)
            w_inter = jnp.exp(log_inter - m_t)
            q_n = jnp.sum(un["q"].astype(F32) * un["n_old"], axis=-1, keepdims=True)
            den = jnp.sum(s, axis=-1, keepdims=True) + w_inter * q_n
            inv = 1.0 / jnp.maximum(jnp.abs(den), jnp.exp(-m_t))
            b_last = b_col[0:1, :] if rev else b_col[c - 1:c, :]
            logw = b_last - b_col + ig_col
            m_new = jnp.maximum(b_last + m_old, jnp.max(logw, axis=0, keepdims=True))
            kw = un["k"].astype(F32) * jnp.exp(logw - m_new)
            un.update(s=s.astype(BF16), w_inter=w_inter, kw=kw, m_new=m_new, inv=inv,
                      decay=jnp.exp(b_last + m_old - m_new))

    tt = (((0,), (0,)), ((), ()))
    for un in units:
        un["sv"] = jnp.dot(un["s"], un["v"], preferred_element_type=F32)
        un["kv"] = lax.dot_general(un["kw"].astype(BF16), un["v"], tt, preferred_element_type=F32)

    n_out, m_out = [], []
    for un in units:
        out_ref = hb_ref if un["rev"] else hf_ref
        hd = un["hd"]
        h = (un["sv"] + un["w_inter"] * un["q_c"]) * un["inv"]
        out_ref[0, :, hd * DV_A:(hd + 1) * DV_A] = h.astype(out_ref.dtype)
        c_scr[un["u"]] = un["decay"] * un["c_old"] + un["kv"]
        n_out.append(un["decay"] * un["n_old"] + jnp.sum(un["kw"], axis=0, keepdims=True))
        m_out.append(jnp.broadcast_to(un["m_new"], (1, LANES)))
    n_scr[...] = jnp.concatenate(n_out, axis=0)
    m_scr[...] = jnp.concatenate(m_out, axis=0)


def _mlstm(qk, proj_h3, gates_col, gates_row, k_meta, meta_h, gates_meta, bias_col, bias_row):
    bsz, t_len, _ = qk.shape
    c = MLSTM_CHUNK
    nc = t_len // c
    va_blk = OFF_VA // D_A
    fwd = lambda b, j: (b, j, 0)
    bwd = lambda b, j: (b, nc - 1 - j, 0)
    return pl.pallas_call(
        _mlstm_kernel,
        grid=(bsz, nc),
        in_specs=[
            pl.BlockSpec((1, c, 2 * QK_A), fwd),
            pl.BlockSpec((1, c, 2 * QK_A), bwd),
            pl.BlockSpec((1, c, D_A), lambda b, j: (b, j, va_blk)),
            pl.BlockSpec((1, c, D_A), lambda b, j: (b, nc - 1 - j, va_blk)),
            pl.BlockSpec((1, c, LANES), fwd),
            pl.BlockSpec((1, c, LANES), bwd),
            pl.BlockSpec((1, 4 * NH_A, c), lambda b, j: (b, 0, j)),
            pl.BlockSpec((1, 4 * NH_A, c), lambda b, j: (b, 0, nc - 1 - j)),
            pl.BlockSpec((1, N_META, QK_A), lambda b, j: (b, 0, 0)),
            pl.BlockSpec((N_META, D_A), lambda b, j: (0, va_blk)),
            pl.BlockSpec((N_META, LANES), lambda b, j: (0, 0)),
            pl.BlockSpec((1, LANES), lambda b, j: (0, 0)),
            pl.BlockSpec((4 * NH_A, 1), lambda b, j: (0, 0)),
        ],
        out_specs=[
            pl.BlockSpec((1, c, D_A), fwd),
            pl.BlockSpec((1, c, D_A), bwd),
        ],
        out_shape=[
            jax.ShapeDtypeStruct((bsz, t_len, D_A), BF16),
            jax.ShapeDtypeStruct((bsz, t_len, D_A), BF16),
        ],
        scratch_shapes=[
            pltpu.VMEM((2 * NH_A, DK_A, DV_A), F32),
            pltpu.VMEM((2 * NH_A, DK_A), F32),
            pltpu.VMEM((2 * NH_A, LANES), F32),
        ],
        compiler_params=_cparams(("arbitrary", "arbitrary")),
        name="mlstm",
    )(qk, qk, proj_h3, proj_h3, gates_col, gates_col, gates_row, gates_row,
      k_meta, meta_h, gates_meta, bias_col, bias_row)


def _natten_col_blocks():
    cols = np.arange(GRID_W)
    cstart = np.clip(cols - KC // 2, 0, GRID_W - KC)
    ustart = np.clip(cstart[::Q_COL_BLOCK], 0, GRID_W - K_COL_SPAN)
    return cols, cstart, ustart


def _natten_bias_table(rpb):
    cols, cstart, ustart = _natten_col_blocks()
    ncb = GRID_W // Q_COL_BLOCK
    qcols = cols.reshape(ncb, Q_COL_BLOCK)[:, :, None]
    kcols = (ustart[:, None] + np.arange(K_COL_SPAN)[None, :])[:, None, :]
    qs = cstart.reshape(ncb, Q_COL_BLOCK)[:, :, None]
    col_valid = (kcols >= qs) & (kcols < qs + KC)
    col_idx = np.clip(kcols - qcols + KC - 1, 0, 2 * KC - 2)
    qr = np.arange(NAT_ROWS)[:, None]
    kk = np.arange(NAT_KROWS)[None, :]
    half = KR_MAX // 2
    rel_start = np.stack([np.maximum(qr - half, 0) + 0 * kk,
                          qr + 0 * kk,
                          np.minimum(qr + half, NAT_KROWS - KR_MAX) + 0 * kk])
    row_delta = np.stack([kk - qr, kk - qr - half, kk - qr - (NAT_KROWS - NAT_ROWS)])
    rpb_c = jnp.where(col_valid[None, None], rpb[:, :, col_idx], NEG_INF)
    rpb_c = rpb_c.transpose(0, 2, 3, 1, 4)
    kinds = []
    for kind in range(3):
        per_row = []
        for r in range(NAT_ROWS):
            first = int(rel_start[kind, r, 0])
            bias_row0 = int(row_delta[kind, r, first]) + KR_MAX - 1
            assert 0 <= bias_row0 and bias_row0 + KR_MAX <= 2 * KR_MAX - 1
            win = rpb_c[:, :, :, bias_row0:bias_row0 + KR_MAX, :]
            per_row.append(jnp.pad(win, ((0, 0), (0, 0), (0, 0), (first, NAT_KROWS - KR_MAX - first), (0, 0)),
                                   constant_values=NEG_INF))
        kinds.append(jnp.stack(per_row, axis=2))
    tab = jnp.stack(kinds, axis=0)
    return tab.reshape(3, NH_B, ncb, NAT_ROWS * Q_COL_BLOCK, NAT_KROWS * K_COL_SPAN)


def _natten_kernel(q_ref, k_ref, v_ref, km_ref, vm_ref, bias_ref, o_ref):
    i = pl.program_id(2)
    rows = k_ref.shape[1]
    ks = jnp.clip(i * NAT_ROWS - KR_MAX // 2, 0, rows - NAT_KROWS)
    _, _, ustart = _natten_col_blocks()
    nq = NAT_ROWS * Q_COL_BLOCK
    nk = NAT_KROWS * K_COL_SPAN
    lane = lax.broadcasted_iota(jnp.int32, (1, LANES), 1)
    kmeta = km_ref[...]
    vmeta = vm_ref[...].astype(BF16)
    nt = (((1,), (1,)), ((), ()))
    sels = [(lane >= h * DH_B) & (lane < (h + 1) * DH_B) for h in range(2)]
    kmh = [jnp.where(sel, kmeta, 0.0).astype(BF16) for sel in sels]
    ncb = GRID_W // Q_COL_BLOCK
    probs = []
    for j in range(ncb):
        u = int(ustart[j])
        qc = slice(j * Q_COL_BLOCK, (j + 1) * Q_COL_BLOCK)
        q = (q_ref[0, :, qc, :].reshape(nq, LANES) * (DH_B ** -0.5)).astype(BF16)
        kwin = k_ref[0, pl.ds(ks, NAT_KROWS), u:u + K_COL_SPAN, :].reshape(nk, LANES)
        vwin = v_ref[0, pl.ds(ks, NAT_KROWS), u:u + K_COL_SPAN, :].reshape(nk, LANES).astype(BF16)
        for h in range(2):
            kh = jnp.where(sels[h], kwin, 0.0).astype(BF16)
            probs.append(dict(
                j=j, h=h, vwin=vwin,
                s=lax.dot_general(q, kh, nt, preferred_element_type=F32),
                sm=lax.dot_general(q, kmh[h], nt, preferred_element_type=F32)))
    for pr in probs:
        s = pr["s"] + bias_ref[0, pr["h"], pr["j"]]
        sm = pr["sm"]
        mx = jnp.maximum(jnp.max(s, -1, keepdims=True), jnp.max(sm, -1, keepdims=True))
        p = jnp.exp(s - mx)
        pm = jnp.exp(sm - mx)
        pr.update(p=p.astype(BF16), pm=pm.astype(BF16),
                  inv=1.0 / (jnp.sum(p, -1, keepdims=True) + jnp.sum(pm, -1, keepdims=True)))
    for pr in probs:
        pr["o"] = (jnp.dot(pr["p"], pr["vwin"], preferred_element_type=F32)
                   + jnp.dot(pr["pm"], vmeta, preferred_element_type=F32)) * pr["inv"]
    for j in range(ncb):
        qc = slice(j * Q_COL_BLOCK, (j + 1) * Q_COL_BLOCK)
        out = jnp.where(sels[0], probs[2 * j]["o"], probs[2 * j + 1]["o"])
        o_ref[0, :, qc, :] = out.reshape(NAT_ROWS, Q_COL_BLOCK, LANES)


def _natten(proj_f3, meta_f, bias_tab):
    bsz, t_len, _ = proj_f3.shape
    rows = t_len // GRID_W
    assert rows % NAT_ROWS == 0 and rows >= NAT_KROWS
    nb = rows // NAT_ROWS
    hp = NH_B // 2
    ncb = GRID_W // Q_COL_BLOCK
    qb, kb, vb = OFF_QB // LANES, OFF_KB // LANES, OFF_VB // LANES
    proj4 = proj_f3.reshape(bsz, rows, GRID_W, P_F32)

    def pat(i):
        return jnp.where(i == 0, 0, jnp.where(i == nb - 1, 2, 1))

    out = pl.pallas_call(
        _natten_kernel,
        grid=(bsz, hp, nb),
        in_specs=[
            pl.BlockSpec((1, NAT_ROWS, GRID_W, LANES), lambda b, p, i: (b, i, 0, qb + p)),
            pl.BlockSpec((1, rows, GRID_W, LANES), lambda b, p, i: (b, 0, 0, kb + p)),
            pl.BlockSpec((1, rows, GRID_W, LANES), lambda b, p, i: (b, 0, 0, vb + p)),
            pl.BlockSpec((N_META, LANES), lambda b, p, i: (0, kb + p)),
            pl.BlockSpec((N_META, LANES), lambda b, p, i: (0, vb + p)),
            pl.BlockSpec((1, 2, ncb, NAT_ROWS * Q_COL_BLOCK, NAT_KROWS * K_COL_SPAN),
                         lambda b, p, i: (pat(i), p, 0, 0, 0)),
        ],
        out_specs=pl.BlockSpec((1, NAT_ROWS, GRID_W, LANES), lambda b, p, i: (b, i, 0, p)),
        out_shape=jax.ShapeDtypeStruct((bsz, rows, GRID_W, D_B), F32),
        compiler_params=_cparams(("arbitrary", "arbitrary", "arbitrary")),
        name="natten",
    )(proj4, proj4, proj4, meta_f, meta_f, bias_tab)
    return out.reshape(bsz, t_len, D_B)


def _merge_kernel(hf_ref, hb_ref, oa_ref, ga_ref, gb_ref, yb_ref, hg_ref, wa_ref, wb_ref, o_ref, ya_scr):
    sub = hf_ref.shape[0] // MERGE_SUBTILES
    tiles = [slice(s * sub, (s + 1) * sub) for s in range(MERGE_SUBTILES)]
    for rows in tiles:
        for hd in range(NH_A):
            sl = slice(hd * DV_A, (hd + 1) * DV_A)
            h = hf_ref[rows, sl].astype(F32) + hb_ref[rows, sl].astype(F32)
            hc = h - jnp.mean(h, -1, keepdims=True)
            var = jnp.mean(hc * hc, -1, keepdims=True)
            hn = hc * lax.rsqrt(var + LN_EPS) * hg_ref[:, sl]
            ya_scr[rows, sl] = (hn * jax.nn.sigmoid(oa_ref[rows, sl].astype(F32))).astype(BF16)
    d = functools.partial(jnp.dot, preferred_element_type=F32)
    proj = [(d(ya_scr[rows, :], wa_ref[...]), d(yb_ref[rows, :].astype(BF16), wb_ref[...])) for rows in tiles]
    for rows, (a, b) in zip(tiles, proj):
        gate_a = jax.nn.sigmoid(ga_ref[rows, :].astype(F32))
        gate_b = jax.nn.sigmoid(gb_ref[rows, :].astype(F32))
        o_ref[rows, :] = (gate_a * a + gate_b * b).astype(BF16)


def _merge(h_f, h_b, proj_h, y_b, head_g, w_a, w_b):
    n = h_f.shape[0]
    tm = MERGE_TM
    const = lambda i: (0, 0)
    return pl.pallas_call(
        _merge_kernel,
        grid=(n // tm,),
        in_specs=[
            pl.BlockSpec((tm, D_A), lambda i: (i, 0)),
            pl.BlockSpec((tm, D_A), lambda i: (i, 0)),
            pl.BlockSpec((tm, D_A), lambda i: (i, OFF_OA // D_A)),
            pl.BlockSpec((tm, D_MODEL), lambda i: (i, OFF_GA // D_MODEL)),
            pl.BlockSpec((tm, D_MODEL), lambda i: (i, OFF_GB // D_MODEL)),
            pl.BlockSpec((tm, D_B), lambda i: (i, 0)),
            pl.BlockSpec((1, D_A), const),
            pl.BlockSpec((D_A, D_MODEL), const, pipeline_mode=pl.Buffered(1)),
            pl.BlockSpec((D_B, D_MODEL), const, pipeline_mode=pl.Buffered(1)),
        ],
        out_specs=pl.BlockSpec((tm, D_MODEL), lambda i: (i, 0)),
        out_shape=jax.ShapeDtypeStruct((n, D_MODEL), BF16),
        scratch_shapes=[pltpu.VMEM((tm, D_A), BF16)],
        compiler_params=_cparams(("arbitrary",)),
        name="merge",
    )(h_f, h_b, proj_h, proj_h, proj_h, y_b, head_g, w_a, w_b)


def _outproj_router_kernel(m_ref, x_ref, lg_ref, lb_ref, wo_ref, g1_ref, b1_ref, wr_hi_ref, wr_lo_ref, br_ref,
                           h1_ref, route_ref):
    tm = m_ref.shape[0]
    sub = tm // OUT_SUBTILES
    d = functools.partial(jnp.dot, preferred_element_type=F32)
    mixes = [d(m_ref[s * sub:(s + 1) * sub, :], wo_ref[...]) for s in range(OUT_SUBTILES)]
    parts = []
    for s in range(OUT_SUBTILES):
        rows = slice(s * sub, (s + 1) * sub)
        h0 = _layer_norm(x_ref[rows, :], lg_ref[...], lb_ref[...])
        h1 = _layer_norm(ALPHA * h0 + mixes[s], g1_ref[...], b1_ref[...])
        h1_ref[rows, :] = h1
        hi = h1.astype(BF16)
        parts.append((hi, (h1 - hi.astype(F32)).astype(BF16)))
    logits = jnp.concatenate(
        [d(hi, wr_hi_ref[...]) + (d(lo, wr_hi_ref[...]) + d(hi, wr_lo_ref[...])) for hi, lo in parts],
        axis=0) + br_ref[...]

    lane = lax.broadcasted_iota(jnp.int32, logits.shape, 1)
    big = jnp.int32(LANES)
    is_g = lane < N_GROUPS
    gl = jnp.where(is_g, logits, NEG_INF)
    gmax = jnp.max(gl, -1, keepdims=True)
    gsel = jnp.min(jnp.where(is_g & (gl == gmax), lane, big), -1, keepdims=True)
    pg = 1.0 / jnp.sum(jnp.where(is_g, jnp.exp(gl - gmax), 0.0), -1, keepdims=True)
    e_lo = N_GROUPS + gsel * EXPERTS_PER_GROUP
    in_grp = (lane >= e_lo) & (lane < e_lo + EXPERTS_PER_GROUP)
    el = jnp.where(in_grp, logits, NEG_INF)
    v0 = jnp.max(el, -1, keepdims=True)
    i0 = jnp.min(jnp.where(in_grp & (el == v0), lane, big), -1, keepdims=True)
    el1 = jnp.where(lane == i0, NEG_INF, el)
    v1 = jnp.max(el1, -1, keepdims=True)
    i1 = jnp.min(jnp.where(in_grp & (lane != i0) & (el1 == v1), lane, big), -1, keepdims=True)
    e1 = jnp.exp(v1 - v0)
    den = 1.0 + e1
    g0 = pg * (1.0 / den)
    g1 = pg * (e1 / den)
    out = jnp.where(lane == 0, (i0 - N_GROUPS).astype(F32),
                    jnp.where(lane == 1, (i1 - N_GROUPS).astype(F32),
                              jnp.where(lane == 2, g0, jnp.where(lane == 3, g1, 0.0))))
    route_ref[...] = out


def _outproj_router(merged, x, ln_in_g, ln_in_b, w_out, ln1_g, ln1_b, wr_hi, wr_lo, br):
    n = x.shape[0]
    tm = OUT_TM
    const = lambda i: (0, 0)
    row = lambda i: (i, 0)
    return pl.pallas_call(
        _outproj_router_kernel,
        grid=(n // tm,),
        in_specs=[
            pl.BlockSpec((tm, D_MODEL), row),
            pl.BlockSpec((tm, D_MODEL), row),
            pl.BlockSpec((1, D_MODEL), const),
            pl.BlockSpec((1, D_MODEL), const),
            pl.BlockSpec((D_MODEL, D_MODEL), const, pipeline_mode=pl.Buffered(1)),
            pl.BlockSpec((1, D_MODEL), const),
            pl.BlockSpec((1, D_MODEL), const),
            pl.BlockSpec((D_MODEL, LANES), const),
            pl.BlockSpec((D_MODEL, LANES), const),
            pl.BlockSpec((1, LANES), const),
        ],
        out_specs=[pl.BlockSpec((tm, D_MODEL), row), pl.BlockSpec((tm, LANES), row)],
        out_shape=[jax.ShapeDtypeStruct((n, D_MODEL), F32), jax.ShapeDtypeStruct((n, LANES), F32)],
        compiler_params=_cparams(("arbitrary",)),
        name="outproj_router",
    )(merged, x, ln_in_g, ln_in_b, w_out, ln1_g, ln1_b, wr_hi, wr_lo, br)


def _moe_ffn_kernel(be_ref, nu_ref, tok_ref, tokn_ref, h1_hbm, wg_ref, wu_ref, wd_ref, y_ref, xbuf, sem):
    i = pl.program_id(0)
    n_used = nu_ref[0]
    bm = xbuf.shape[1]
    slot = lax.rem(i, 2)
    nxt = 1 - slot

    def start_rows(t_ref, dst_slot, r0, r1):
        for r in range(r0, r1):
            pltpu.make_async_copy(h1_hbm.at[pl.ds(t_ref[0, 0, r], 1)], xbuf.at[dst_slot, pl.ds(r, 1)],
                                  sem.at[dst_slot]).start()

    def wait_rows(s):
        pltpu.make_async_copy(h1_hbm.at[pl.ds(0, bm)], xbuf.at[s], sem.at[s]).wait()

    @pl.when(i == 0)
    def _():
        start_rows(tok_ref, 0, 0, bm)

    @pl.when(i < n_used)
    def _():
        start_rows(tokn_ref, nxt, 0, bm)

    @pl.when(i < n_used)
    def _():
        wait_rows(slot)
        xb = xbuf[slot].astype(BF16)
        tn = (((0,), (1,)), ((), ()))
        tt = (((0,), (0,)), ((), ()))
        g_t = lax.dot_general(wg_ref[0], xb, tn, preferred_element_type=F32)
        u_t = lax.dot_general(wu_ref[0], xb, tn, preferred_element_type=F32)
        h_t = (jax.nn.silu(g_t) * u_t).astype(BF16)
        for half in range(2):
            sl = slice(half * D_EXPERT, (half + 1) * D_EXPERT)
            y_ref[:, sl] = lax.dot_general(wd_ref[0, :, sl], h_t, tt, preferred_element_type=F32).T

    @pl.when(i >= n_used)
    def _():
        @pl.when(i == n_used)
        def _():
            wait_rows(slot)

        y_ref[...] = jnp.zeros(y_ref.shape, F32)


def _moe_ffn(block_e, n_used, tok3, h1, w_gate, w_up, w_down):
    nb = tok3.shape[0]
    bm = MOE_BM
    wmap = lambda i, be, nu: (be[i], 0, 0)
    grid_spec = pltpu.PrefetchScalarGridSpec(
        num_scalar_prefetch=2,
        grid=(nb,),
        in_specs=[
            pl.BlockSpec((1, 1, bm), lambda i, be, nu: (i, 0, 0), memory_space=pltpu.SMEM),
            pl.BlockSpec((1, 1, bm), lambda i, be, nu: (jnp.minimum(i + 1, nb - 1), 0, 0),
                         memory_space=pltpu.SMEM),
            pl.BlockSpec(memory_space=pl.ANY),
            pl.BlockSpec((1, D_MODEL, D_EXPERT), wmap),
            pl.BlockSpec((1, D_MODEL, D_EXPERT), wmap),
            pl.BlockSpec((1, D_EXPERT, D_MODEL), wmap),
        ],
        out_specs=pl.BlockSpec((bm, D_MODEL), lambda i, be, nu: (i, 0)),
        scratch_shapes=[pltpu.VMEM((2, bm, D_MODEL), F32), pltpu.SemaphoreType.DMA((2,))],
    )
    return pl.pallas_call(
        _moe_ffn_kernel,
        grid_spec=grid_spec,
        out_shape=jax.ShapeDtypeStruct((nb * bm, D_MODEL), F32),
        compiler_params=_cparams(("arbitrary",)),
        name="moe_ffn",
    )(block_e, n_used, tok3, tok3, h1, w_gate, w_up, w_down)


def _combine_kernel(pos_ref, posn_ref, y_hbm, h1_ref, route_ref, g_ref, b_ref, o_ref, buf, sem):
    i = pl.program_id(0)
    tm = buf.shape[2]
    slot = lax.rem(i, 2)

    def start_rows(p_ref, s):
        for r in range(tm):
            for k in range(2):
                pltpu.make_async_copy(y_hbm.at[pl.ds(p_ref[0, 0, 2 * r + k], 1)],
                                      buf.at[s, k, pl.ds(r, 1)], sem.at[s]).start()

    @pl.when(i == 0)
    def _():
        start_rows(pos_ref, 0)

    @pl.when(i + 1 < pl.num_programs(0))
    def _():
        start_rows(posn_ref, 1 - slot)

    for k in range(2):
        pltpu.make_async_copy(y_hbm.at[pl.ds(0, tm)], buf.at[slot, k], sem.at[slot]).wait()
    route = route_ref[...]
    ffn = buf[slot, 0] * route[:, 2:3] + buf[slot, 1] * route[:, 3:4]
    o_ref[...] = _layer_norm(ALPHA * h1_ref[...] + ffn, g_ref[...], b_ref[...])


def _combine(pos3, y_sorted, h1, route, ln2_g, ln2_b):
    n = h1.shape[0]
    tm = COMB_TM
    nt = n // tm
    const = lambda i: (0, 0)
    row = lambda i: (i, 0)
    return pl.pallas_call(
        _combine_kernel,
        grid=(nt,),
        in_specs=[
            pl.BlockSpec((1, 1, 2 * tm), lambda i: (i, 0, 0), memory_space=pltpu.SMEM),
            pl.BlockSpec((1, 1, 2 * tm), lambda i: (jnp.minimum(i + 1, nt - 1), 0, 0), memory_space=pltpu.SMEM),
            pl.BlockSpec(memory_space=pl.ANY),
            pl.BlockSpec((tm, D_MODEL), row),
            pl.BlockSpec((tm, LANES), row),
            pl.BlockSpec((1, D_MODEL), const),
            pl.BlockSpec((1, D_MODEL), const),
        ],
        out_specs=pl.BlockSpec((tm, D_MODEL), row),
        out_shape=jax.ShapeDtypeStruct((n, D_MODEL), F32),
        scratch_shapes=[pltpu.VMEM((2, 2, tm, D_MODEL), F32), pltpu.SemaphoreType.DMA((2,))],
        compiler_params=_cparams(("arbitrary",)),
        name="combine",
    )(pos3, pos3, y_sorted, h1, route, ln2_g, ln2_b)


def _rank_kernel(route_ref, pstart_ref, pos_ref, carry):
    @pl.when(pl.program_id(0) == 0)
    def _():
        carry[...] = jnp.zeros(carry.shape, F32)

    tm = route_ref.shape[0]
    route = route_ref[...]
    lane = lax.broadcasted_iota(jnp.int32, (tm, LANES), 1)
    oh0 = lane == route[:, 0:1].astype(jnp.int32)
    oh1 = lane == route[:, 1:2].astype(jnp.int32)
    cnt = jnp.where(oh0, 1.0, 0.0) + jnp.where(oh1, 1.0, 0.0)
    r_i = lax.broadcasted_iota(jnp.int32, (tm, tm), 0)
    c_i = lax.broadcasted_iota(jnp.int32, (tm, tm), 1)
    earlier = jnp.dot((c_i < r_i).astype(BF16), cnt.astype(BF16), preferred_element_type=F32)
    row = earlier + (carry[...] + pstart_ref[...])
    pos0 = jnp.sum(jnp.where(oh0, row, 0.0), -1, keepdims=True)
    pos1 = jnp.sum(jnp.where(oh1, row, 0.0), -1, keepdims=True)
    pos_ref[...] = jnp.where(lane == 0, pos0, jnp.where(lane == 1, pos1, 0.0)).astype(jnp.int32)
    carry[...] += jnp.sum(cnt, axis=0, keepdims=True)


def _rank(route, pstart_row):
    n = route.shape[0]
    tm = RANK_TM
    return pl.pallas_call(
        _rank_kernel,
        grid=(n // tm,),
        in_specs=[pl.BlockSpec((tm, LANES), lambda i: (i, 0)), pl.BlockSpec((1, LANES), lambda i: (0, 0))],
        out_specs=pl.BlockSpec((tm, LANES), lambda i: (i, 0)),
        out_shape=jax.ShapeDtypeStruct((n, LANES), jnp.int32),
        scratch_shapes=[pltpu.VMEM((1, LANES), F32)],
        compiler_params=_cparams(("arbitrary",)),
        name="rank",
    )(route, pstart_row)


def _dispatch_plan(route, n):
    bm = MOE_BM
    a = 2 * n
    nb = a // bm + N_EXPERTS
    id_bits = max(a - 1, 1).bit_length()
    flat_e = route[:, 0:2].astype(jnp.int32).reshape(a)
    ids = jnp.arange(a, dtype=jnp.int32)
    order = jnp.sort(flat_e * (1 << id_bits) + ids) & ((1 << id_bits) - 1)
    experts = jnp.arange(N_EXPERTS, dtype=jnp.int32)
    counts = jnp.sum((flat_e[:, None] == experts[None, :]).astype(jnp.int32), axis=0)
    pcounts = (counts + bm - 1) // bm * bm
    incl = (experts[:, None] <= experts[None, :]).astype(jnp.int32)
    pend = pcounts @ incl
    pstart = pend - pcounts
    cstart = counts @ incl - counts
    blk0 = jnp.arange(nb, dtype=jnp.int32) * bm
    block_e = jnp.minimum(jnp.sum((pend[None, :] <= blk0[:, None]).astype(jnp.int32), axis=1),
                          N_EXPERTS - 1)
    off = (blk0 - pstart[block_e])[:, None] + jnp.arange(bm, dtype=jnp.int32)[None, :]
    valid = off < counts[block_e][:, None]
    src = jnp.clip(cstart[block_e][:, None] + off, 0, a - 1)
    tok = jnp.where(valid, order[src] >> 1, 0)
    n_used = (pend[-1:] // bm).astype(jnp.int32)
    pstart_row = jnp.pad(pstart.astype(F32), (0, LANES - N_EXPERTS)).reshape(1, LANES)
    pos = _rank(route, pstart_row)[:, :2]
    return (block_e.astype(jnp.int32), n_used, tok.reshape(nb, 1, bm),
            pos.reshape(n // COMB_TM, 1, 2 * COMB_TM))


def _encode(x, p):
    bsz, t_len, _ = x.shape
    n = bsz * t_len
    x2 = x.reshape(n, D_MODEL)
    proj_f, proj_h, gates = _inproj(x2, p["ln_in_g"], p["ln_in_b"], p["w_main"], p["w_gates"])
    proj_f3 = proj_f.reshape(bsz, t_len, P_F32)
    proj_h3 = proj_h.reshape(bsz, t_len, P_BF16)
    gates3 = gates.reshape(bsz, t_len, LANES)
    gates_row = gates3[:, :, :4 * NH_A].transpose(0, 2, 1)
    qk, k_meta = _conv(proj_f3, p["meta_f"], p["conv_w"], p["conv_b"])
    h_f, h_b = _mlstm(qk, proj_h3, gates3, gates_row, k_meta, p["meta_h"], p["gates_meta"],
                      p["bias_col"], p["bias_row"])
    y_b = _natten(proj_f3, p["meta_f"], p["bias_tab"])
    merged = _merge(h_f.reshape(n, D_A), h_b.reshape(n, D_A), proj_h, y_b.reshape(n, D_B),
                    p["head_g"], p["w_a"], p["w_b"])
    h1, route = _outproj_router(merged, x2, p["ln_in_g"], p["ln_in_b"], p["w_out"], p["ln1_g"], p["ln1_b"],
                                p["wr_hi"], p["wr_lo"], p["br"])
    block_e, n_used, tok3, pos3 = _dispatch_plan(route, n)
    y_sorted = _moe_ffn(block_e, n_used, tok3, h1, p["w_gate"], p["w_up"], p["w_down"])
    out = _combine(pos3, y_sorted, h1, route, p["ln2_g"], p["ln2_b"])
    return out.reshape(bsz, t_len, D_MODEL)


def kernel(x_prompt, x_sample, meta_tokens, ln_in_g, ln_in_b, w_in, b_gates, conv_w, conv_b, head_g, rpb, w_a, w_b, w_out, ln1_g, ln1_b, router_g_w, router_g_b, router_e_w, router_e_b, w_gate, w_up, w_down, ln2_g, ln2_b):
    l = 0
    offs = np.cumsum((0,) + IN_SPLITS)
    col = lambda i: w_in[l][:, offs[i]:offs[i + 1]]
    w_main = jnp.concatenate([col(0), col(1), col(5), col(6), col(7), col(2), col(3), col(8), col(9)],
                             axis=1).astype(BF16)
    w_gates = jnp.pad(col(4), ((0, 0), (0, LANES - 4 * NH_A))).astype(BF16)
    row = lambda v: v.reshape(1, -1).astype(F32)
    wr = jnp.pad(jnp.concatenate([router_g_w[l], router_e_w[l]], axis=1),
                 ((0, 0), (0, LANES - N_GROUPS - N_EXPERTS)))
    wr_hi = wr.astype(BF16)
    p = {
        "ln_in_g": row(ln_in_g), "ln_in_b": row(ln_in_b),
        "w_main": w_main, "w_gates": w_gates,
        "conv_w": conv_w[l], "conv_b": row(conv_b[l]),
        "bias_col": jnp.pad(b_gates[l], (0, LANES - 4 * NH_A)).reshape(1, LANES),
        "bias_row": b_gates[l].reshape(4 * NH_A, 1),
        "head_g": row(head_g[l]),
        "bias_tab": _natten_bias_table(rpb[l]),
        "w_a": w_a[l].astype(BF16), "w_b": w_b[l].astype(BF16), "w_out": w_out[l].astype(BF16),
        "ln1_g": row(ln1_g[l]), "ln1_b": row(ln1_b[l]),
        "wr_hi": wr_hi, "wr_lo": (wr - wr_hi.astype(F32)).astype(BF16),
        "br": jnp.pad(jnp.concatenate([router_g_b[l], router_e_b[l]]),
                      (0, LANES - N_GROUPS - N_EXPERTS)).reshape(1, LANES),
        "w_gate": w_gate[l].astype(BF16), "w_up": w_up[l].astype(BF16), "w_down": w_down[l].astype(BF16),
        "ln2_g": row(ln2_g[l]), "ln2_b": row(ln2_b[l]),
    }
    p["meta_f"], p["meta_h"], p["gates_meta"] = _inproj(meta_tokens, p["ln_in_g"], p["ln_in_b"], w_main, w_gates)
    return (_encode(x_prompt, p), _encode(x_sample, p))
```

```python
import functools

import jax
import jax.numpy as jnp
import numpy as np
from jax import lax
from jax.experimental import pallas as pl
from jax.experimental.pallas import tpu as pltpu

F32 = jnp.float32
BF16 = jnp.bfloat16

D_MODEL = 2048
N_META = 16
GRID_W = 64
NH_A = 8
DK_A = 128
DV_A = 256
QK_A = NH_A * DK_A
D_A = NH_A * DV_A
NH_B = 16
DH_B = 64
D_B = NH_B * DH_B
KR_MAX = 8
KC = 16
Q_COL_BLOCK = 16
K_COL_SPAN = 32
N_GROUPS = 4
EXPERTS_PER_GROUP = 8
N_EXPERTS = N_GROUPS * EXPERTS_PER_GROUP
D_EXPERT = 1024
DEPTH = 1
ALPHA = (2 * DEPTH) ** 0.25
LN_EPS = 1e-5
NEG_INF = -1e30
IN_SPLITS = (QK_A, QK_A, D_A, D_A, 4 * NH_A, D_B, D_B, D_B, D_MODEL, D_MODEL)

OFF_QK = 0
OFF_QB = 2048
OFF_KB = 3072
OFF_VB = 4096
P_F32 = 5120
OFF_VA = 0
OFF_OA = 2048
OFF_GA = 4096
OFF_GB = 6144
P_BF16 = 8192
P_MAIN = P_F32 + P_BF16
LANES = 128

VMEM_LIMIT = 56 * 1024 * 1024

INPROJ_TM = 1024
INPROJ_TN = 1024
INPROJ_LN_SLAB = 256
CONV_TC = 512
MLSTM_CHUNK = 256
NAT_ROWS = 8
NAT_KROWS = 16
MERGE_TM = 512
MERGE_SUBTILES = 2
OUT_TM = 512
OUT_SUBTILES = 2
MOE_BM = 256
COMB_TM = 256
RANK_TM = 512


def _cparams(sem):
    return pltpu.CompilerParams(dimension_semantics=sem, vmem_limit_bytes=VMEM_LIMIT)


def _layer_norm(x, g, b):
    xc = x - jnp.mean(x, -1, keepdims=True)
    var = jnp.mean(xc * xc, -1, keepdims=True)
    return xc * lax.rsqrt(var + LN_EPS) * g + b


def _inproj_kernel(x_ref, g_ref, b_ref, w_ref, wg_ref, of_ref, oh_ref, og_ref, h_scr):
    @pl.when(pl.program_id(1) == 0)
    def _():
        tm = x_ref.shape[0]
        slab = min(INPROJ_LN_SLAB, tm)

        def ln_slab(s, carry):
            r = pl.multiple_of(s * slab, slab)
            h = _layer_norm(x_ref[pl.ds(r, slab), :], g_ref[...], b_ref[...]).astype(BF16)
            h_scr[pl.ds(r, slab), :] = h
            og_ref[pl.ds(r, slab), :] = jnp.dot(h, wg_ref[...], preferred_element_type=F32)
            return carry

        lax.fori_loop(0, tm // slab, ln_slab, 0)

    n_f32 = P_F32 // w_ref.shape[1]

    @pl.when(pl.program_id(1) < n_f32)
    def _():
        of_ref[...] = jnp.dot(h_scr[...], w_ref[...], preferred_element_type=F32)

    @pl.when(pl.program_id(1) >= n_f32)
    def _():
        oh_ref[...] = jnp.dot(h_scr[...], w_ref[...], preferred_element_type=F32).astype(BF16)


def _inproj(x, ln_g, ln_b, w_main, w_gates):
    n = x.shape[0]
    tm = min(INPROJ_TM, n)
    tn = INPROJ_TN
    n_f32 = P_F32 // tn
    return pl.pallas_call(
        _inproj_kernel,
        grid=(n // tm, P_MAIN // tn),
        in_specs=[
            pl.BlockSpec((tm, D_MODEL), lambda i, j: (i, 0)),
            pl.BlockSpec((1, D_MODEL), lambda i, j: (0, 0)),
            pl.BlockSpec((1, D_MODEL), lambda i, j: (0, 0)),
            pl.BlockSpec((D_MODEL, tn), lambda i, j: (0, j)),
            pl.BlockSpec((D_MODEL, LANES), lambda i, j: (0, 0)),
        ],
        out_specs=[
            pl.BlockSpec((tm, tn), lambda i, j: (i, jnp.minimum(j, n_f32 - 1))),
            pl.BlockSpec((tm, tn), lambda i, j: (i, jnp.maximum(j - n_f32, 0))),
            pl.BlockSpec((tm, LANES), lambda i, j: (i, 0)),
        ],
        out_shape=[
            jax.ShapeDtypeStruct((n, P_F32), F32),
            jax.ShapeDtypeStruct((n, P_BF16), BF16),
            jax.ShapeDtypeStruct((n, LANES), F32),
        ],
        scratch_shapes=[pltpu.VMEM((tm, D_MODEL), BF16)],
        compiler_params=_cparams(("arbitrary", "arbitrary")),
        name="inproj",
    )(x, ln_g, ln_b, w_main, w_gates)


def _conv_kernel(x_ref, prev_ref, next_ref, meta_ref, w_ref, b_ref, qk_ref, kmeta_ref):
    t = pl.program_id(1)
    last = pl.num_programs(1) - 1
    tc = x_ref.shape[1]
    w0, w1, w2 = w_ref[0:1, :], w_ref[1:2, :], w_ref[2:3, :]
    bias = b_ref[...]
    lane = lax.broadcasted_iota(jnp.int32, (1, 2 * QK_A), 1)
    out_scale = jnp.where(lane >= QK_A, DK_A ** -0.5, 1.0).astype(F32)

    x = x_ref[0]
    meta = meta_ref[...]
    prev_row = jnp.where(t == 0, meta[N_META - 1:N_META, :], prev_ref[0, 7:8, :])
    next_row = jnp.where(t == last, jnp.zeros_like(prev_row), next_ref[0, 0:1, :])
    row = lax.broadcasted_iota(jnp.int32, (tc, 1), 0)
    x_prev = jnp.where(row == 0, prev_row, pltpu.roll(x, 1, 0))
    x_next = jnp.where(row == tc - 1, next_row, pltpu.roll(x, tc - 1, 0))
    y = x_prev * w0 + x * w1 + x_next * w2 + bias
    qk_ref[0] = (jax.nn.silu(y) * out_scale).astype(BF16)

    @pl.when(t == 0)
    def _():
        mrow = lax.broadcasted_iota(jnp.int32, (N_META, 1), 0)
        m_prev = jnp.where(mrow == 0, 0.0, pltpu.roll(meta, 1, 0))
        m_next = jnp.where(mrow == N_META - 1, x[0:1, :], pltpu.roll(meta, N_META - 1, 0))
        ym = m_prev * w0 + meta * w1 + m_next * w2 + bias
        km = jax.nn.silu(ym) * out_scale
        kmeta_ref[0] = km[:, QK_A:].astype(BF16)


def _conv(proj_f3, meta_f, conv_w, conv_b):
    bsz, t_len, _ = proj_f3.shape
    tc = CONV_TC
    nt = t_len // tc
    r8 = tc // 8
    return pl.pallas_call(
        _conv_kernel,
        grid=(bsz, nt),
        in_specs=[
            pl.BlockSpec((1, tc, 2 * QK_A), lambda b, t: (b, t, 0)),
            pl.BlockSpec((1, 8, 2 * QK_A), lambda b, t: (b, jnp.maximum(t * r8 - 1, 0), 0)),
            pl.BlockSpec((1, 8, 2 * QK_A), lambda b, t: (b, jnp.minimum((t + 1) * r8, t_len // 8 - 1), 0)),
            pl.BlockSpec((N_META, 2 * QK_A), lambda b, t: (0, 0)),
            pl.BlockSpec((3, 2 * QK_A), lambda b, t: (0, 0)),
            pl.BlockSpec((1, 2 * QK_A), lambda b, t: (0, 0)),
        ],
        out_specs=[
            pl.BlockSpec((1, tc, 2 * QK_A), lambda b, t: (b, t, 0)),
            pl.BlockSpec((1, N_META, QK_A), lambda b, t: (b, 0, 0)),
        ],
        out_shape=[
            jax.ShapeDtypeStruct((bsz, t_len, 2 * QK_A), BF16),
            jax.ShapeDtypeStruct((bsz, N_META, QK_A), BF16),
        ],
        compiler_params=_cparams(("arbitrary", "arbitrary")),
        name="conv_silu",
    )(proj_f3, proj_f3, proj_f3, meta_f, conv_w, conv_b)


def _split3(x):
    hi = x.astype(BF16)
    r1 = x - hi.astype(F32)
    mid = r1.astype(BF16)
    lo = (r1 - mid.astype(F32)).astype(BF16)
    return hi, mid, lo


def _tri_left(tri, x):
    hi, mid, lo = _split3(x)
    d = functools.partial(jnp.dot, preferred_element_type=F32)
    return d(tri, hi) + d(tri, mid) + d(tri, lo)


def _tri_right(x, tri):
    hi, mid, lo = _split3(x)
    d = functools.partial(jnp.dot, preferred_element_type=F32)
    return d(hi, tri) + d(mid, tri) + d(lo, tri)


def _state_update(k, v, ig_col, b_col, b_last, c_old, n_old, m_old):
    logw = b_last - b_col + ig_col
    m_new = jnp.maximum(b_last + m_old, jnp.max(logw, axis=0, keepdims=True))
    w = jnp.exp(logw - m_new)
    decay = jnp.exp(b_last + m_old - m_new)
    kw = k.astype(F32) * w
    kv = lax.dot_general(kw.astype(BF16), v, (((0,), (0,)), ((), ())), preferred_element_type=F32)
    c_new = decay * c_old + kv
    n_new = decay * n_old + jnp.sum(kw, axis=0, keepdims=True)
    return c_new, n_new, m_new


def _mlstm_kernel(qkf_ref, qkb_ref, vf_ref, vb_ref, gcf_ref, gcb_ref, grf_ref, grb_ref,
                  kmeta_ref, vmeta_ref, gmeta_ref, bcol_ref, brow_ref,
                  hf_ref, hb_ref, c_scr, n_scr, m_scr):
    j = pl.program_id(1)
    c = qkf_ref.shape[1]
    r_i = lax.broadcasted_iota(jnp.int32, (c, c), 0)
    c_i = lax.broadcasted_iota(jnp.int32, (c, c), 1)
    lower = r_i >= c_i
    upper = r_i <= c_i
    tri_l = lower.astype(BF16)
    tri_u = upper.astype(BF16)

    @pl.when(j == 0)
    def _():
        gm = gmeta_ref[...] + bcol_ref[...]
        lfm = jax.nn.log_sigmoid(gm)
        mr = lax.broadcasted_iota(jnp.int32, (N_META, N_META), 0)
        mc = lax.broadcasted_iota(jnp.int32, (N_META, N_META), 1)
        bm = _tri_left((mr >= mc).astype(BF16), lfm)
        zero_m = jnp.zeros((1, 1), F32)
        for hd in range(NH_A):
            ig_col = gm[:, hd:hd + 1]
            b_col = bm[:, 2 * NH_A + hd:2 * NH_A + hd + 1]
            b_last = b_col[N_META - 1:N_META, :]
            k = kmeta_ref[0, :, hd * DK_A:(hd + 1) * DK_A]
            v = vmeta_ref[:, hd * DV_A:(hd + 1) * DV_A].astype(BF16)
            c_new, n_new, m_new = _state_update(
                k, v, ig_col, b_col, b_last,
                jnp.zeros((DK_A, DV_A), F32), jnp.zeros((1, DK_A), F32), zero_m)
            c_scr[hd] = c_new
            n_scr[hd:hd + 1, :] = n_new
            m_scr[hd:hd + 1, :] = jnp.broadcast_to(m_new, (1, LANES))
        for hd in range(NH_A):
            u = NH_A + hd
            c_scr[u] = jnp.zeros((DK_A, DV_A), F32)
            n_scr[u:u + 1, :] = jnp.zeros((1, DK_A), F32)
            m_scr[u:u + 1, :] = jnp.zeros((1, LANES), F32)

    n_all = n_scr[...]
    m_all = m_scr[...]
    nt = (((1,), (1,)), ((), ()))
    units = []
    for direction in range(2):
        rev = direction == 1
        qk_ref = qkb_ref if rev else qkf_ref
        v_ref = vb_ref if rev else vf_ref
        for hd in range(NH_A):
            u = direction * NH_A + hd
            q = qk_ref[0, :, hd * DK_A:(hd + 1) * DK_A]
            k = qk_ref[0, :, QK_A + hd * DK_A:QK_A + (hd + 1) * DK_A]
            v = v_ref[0, :, hd * DV_A:(hd + 1) * DV_A].astype(BF16)
            c_old = c_scr[u]
            units.append(dict(
                rev=rev, hd=hd, u=u, q=q, k=k, v=v, c_old=c_old,
                n_old=n_all[u:u + 1, :], m_old=m_all[u:u + 1, 0:1],
                s_raw=lax.dot_general(q, k, nt, preferred_element_type=F32),
                q_c=jnp.dot(q, c_old.astype(BF16), preferred_element_type=F32)))

    for direction in range(2):
        rev = direction == 1
        gc = (gcb_ref if rev else gcf_ref)[0] + bcol_ref[...]
        gr = (grb_ref if rev else grf_ref)[0] + brow_ref[...]
        lf_c = jax.nn.log_sigmoid(gc)
        lf_r = jax.nn.log_sigmoid(gr)
        if rev:
            bc_all = _tri_left(tri_u, lf_c)
            br_all = _tri_right(lf_r, tri_l)
            mask = upper
        else:
            bc_all = _tri_left(tri_l, lf_c)
            br_all = _tri_right(lf_r, tri_u)
            mask = lower
        for hd in range(NH_A):
            un = units[direction * NH_A + hd]
            ig_lane = direction * NH_A + hd
            lf_lane = 2 * NH_A + direction * NH_A + hd
            ig_col = gc[:, ig_lane:ig_lane + 1]
            b_col = bc_all[:, lf_lane:lf_lane + 1]
            ig_row = gr[ig_lane:ig_lane + 1, :]
            b_row = br_all[lf_lane:lf_lane + 1, :]
            m_old = un["m_old"]
            logd = jnp.where(mask, b_col + (ig_row - b_row), NEG_INF)
            log_inter = b_col + m_old
            m_t = jnp.maximum(log_inter, jnp.max(logd, axis=-1, keepdims=True))
            s = un["s_raw"] * jnp.exp(logd - m_t)
            w_inter = jnp.exp(log_inter - m_t)
            q_n = jnp.sum(un["q"].astype(F32) * un["n_old"], axis=-1, keepdims=True)
            den = jnp.sum(s, axis=-1, keepdims=True) + w_inter * q_n
            inv = 1.0 / jnp.maximum(jnp.abs(den), jnp.exp(-m_t))
            b_last = b_col[0:1, :] if rev else b_col[c - 1:c, :]
            logw = b_last - b_col + ig_col
            m_new = jnp.maximum(b_last + m_old, jnp.max(logw, axis=0, keepdims=True))
            kw = un["k"].astype(F32) * jnp.exp(logw - m_new)
            un.update(s=s.astype(BF16), w_inter=w_inter, kw=kw, m_new=m_new, inv=inv,
                      decay=jnp.exp(b_last + m_old - m_new))

    tt = (((0,), (0,)), ((), ()))
    for un in units:
        un["sv"] = jnp.dot(un["s"], un["v"], preferred_element_type=F32)
        un["kv"] = lax.dot_general(un["kw"].astype(BF16), un["v"], tt, preferred_element_type=F32)

    n_out, m_out = [], []
    for un in units:
        out_ref = hb_ref if un["rev"] else hf_ref
        hd = un["hd"]
        h = (un["sv"] + un["w_inter"] * un["q_c"]) * un["inv"]
        out_ref[0, :, hd * DV_A:(hd + 1) * DV_A] = h.astype(out_ref.dtype)
        c_scr[un["u"]] = un["decay"] * un["c_old"] + un["kv"]
        n_out.append(un["decay"] * un["n_old"] + jnp.sum(un["kw"], axis=0, keepdims=True))
        m_out.append(jnp.broadcast_to(un["m_new"], (1, LANES)))
    n_scr[...] = jnp.concatenate(n_out, axis=0)
    m_scr[...] = jnp.concatenate(m_out, axis=0)


def _mlstm(qk, proj_h3, gates_col, gates_row, k_meta, meta_h, gates_meta, bias_col, bias_row):
    bsz, t_len, _ = qk.shape
    c = MLSTM_CHUNK
    nc = t_len // c
    va_blk = OFF_VA // D_A
    fwd = lambda b, j: (b, j, 0)
    bwd = lambda b, j: (b, nc - 1 - j, 0)
    return pl.pallas_call(
        _mlstm_kernel,
        grid=(bsz, nc),
        in_specs=[
            pl.BlockSpec((1, c, 2 * QK_A), fwd),
            pl.BlockSpec((1, c, 2 * QK_A), bwd),
            pl.BlockSpec((1, c, D_A), lambda b, j: (b, j, va_blk)),
            pl.BlockSpec((1, c, D_A), lambda b, j: (b, nc - 1 - j, va_blk)),
            pl.BlockSpec((1, c, LANES), fwd),
            pl.BlockSpec((1, c, LANES), bwd),
            pl.BlockSpec((1, 4 * NH_A, c), lambda b, j: (b, 0, j)),
            pl.BlockSpec((1, 4 * NH_A, c), lambda b, j: (b, 0, nc - 1 - j)),
            pl.BlockSpec((1, N_META, QK_A), lambda b, j: (b, 0, 0)),
            pl.BlockSpec((N_META, D_A), lambda b, j: (0, va_blk)),
            pl.BlockSpec((N_META, LANES), lambda b, j: (0, 0)),
            pl.BlockSpec((1, LANES), lambda b, j: (0, 0)),
            pl.BlockSpec((4 * NH_A, 1), lambda b, j: (0, 0)),
        ],
        out_specs=[
            pl.BlockSpec((1, c, D_A), fwd),
            pl.BlockSpec((1, c, D_A), bwd),
        ],
        out_shape=[
            jax.ShapeDtypeStruct((bsz, t_len, D_A), BF16),
            jax.ShapeDtypeStruct((bsz, t_len, D_A), BF16),
        ],
        scratch_shapes=[
            pltpu.VMEM((2 * NH_A, DK_A, DV_A), F32),
            pltpu.VMEM((2 * NH_A, DK_A), F32),
            pltpu.VMEM((2 * NH_A, LANES), F32),
        ],
        compiler_params=_cparams(("arbitrary", "arbitrary")),
        name="mlstm",
    )(qk, qk, proj_h3, proj_h3, gates_col, gates_col, gates_row, gates_row,
      k_meta, meta_h, gates_meta, bias_col, bias_row)


def _natten_col_blocks():
    cols = np.arange(GRID_W)
    cstart = np.clip(cols - KC // 2, 0, GRID_W - KC)
    ustart = np.clip(cstart[::Q_COL_BLOCK], 0, GRID_W - K_COL_SPAN)
    return cols, cstart, ustart


def _natten_bias_table(rpb):
    cols, cstart, ustart = _natten_col_blocks()
    ncb = GRID_W // Q_COL_BLOCK
    qcols = cols.reshape(ncb, Q_COL_BLOCK)[:, :, None]
    kcols = (ustart[:, None] + np.arange(K_COL_SPAN)[None, :])[:, None, :]
    qs = cstart.reshape(ncb, Q_COL_BLOCK)[:, :, None]
    col_valid = (kcols >= qs) & (kcols < qs + KC)
    col_idx = np.clip(kcols - qcols + KC - 1, 0, 2 * KC - 2)
    qr = np.arange(NAT_ROWS)[:, None]
    kk = np.arange(NAT_KROWS)[None, :]
    half = KR_MAX // 2
    rel_start = np.stack([np.maximum(qr - half, 0) + 0 * kk,
                          qr + 0 * kk,
                          np.minimum(qr + half, NAT_KROWS - KR_MAX) + 0 * kk])
    row_delta = np.stack([kk - qr, kk - qr - half, kk - qr - (NAT_KROWS - NAT_ROWS)])
    rpb_c = jnp.where(col_valid[None, None], rpb[:, :, col_idx], NEG_INF)
    rpb_c = rpb_c.transpose(0, 2, 3, 1, 4)
    kinds = []
    for kind in range(3):
        per_row = []
        for r in range(NAT_ROWS):
            first = int(rel_start[kind, r, 0])
            bias_row0 = int(row_delta[kind, r, first]) + KR_MAX - 1
            assert 0 <= bias_row0 and bias_row0 + KR_MAX <= 2 * KR_MAX - 1
            win = rpb_c[:, :, :, bias_row0:bias_row0 + KR_MAX, :]
            per_row.append(jnp.pad(win, ((0, 0), (0, 0), (0, 0), (first, NAT_KROWS - KR_MAX - first), (0, 0)),
                                   constant_values=NEG_INF))
        kinds.append(jnp.stack(per_row, axis=2))
    tab = jnp.stack(kinds, axis=0)
    return tab.reshape(3, NH_B, ncb, NAT_ROWS * Q_COL_BLOCK, NAT_KROWS * K_COL_SPAN)


def _natten_kernel(q_ref, k_ref, v_ref, km_ref, vm_ref, bias_ref, o_ref):
    i = pl.program_id(2)
    rows = k_ref.shape[1]
    ks = jnp.clip(i * NAT_ROWS - KR_MAX // 2, 0, rows - NAT_KROWS)
    _, _, ustart = _natten_col_blocks()
    nq = NAT_ROWS * Q_COL_BLOCK
    nk = NAT_KROWS * K_COL_SPAN
    lane = lax.broadcasted_iota(jnp.int32, (1, LANES), 1)
    kmeta = km_ref[...]
    vmeta = vm_ref[...].astype(BF16)
    nt = (((1,), (1,)), ((), ()))
    sels = [(lane >= h * DH_B) & (lane < (h + 1) * DH_B) for h in range(2)]
    kmh = [jnp.where(sel, kmeta, 0.0).astype(BF16) for sel in sels]
    ncb = GRID_W // Q_COL_BLOCK
    probs = []
    for j in range(ncb):
        u = int(ustart[j])
        qc = slice(j * Q_COL_BLOCK, (j + 1) * Q_COL_BLOCK)
        q = (q_ref[0, :, qc, :].reshape(nq, LANES) * (DH_B ** -0.5)).astype(BF16)
        kwin = k_ref[0, pl.ds(ks, NAT_KROWS), u:u + K_COL_SPAN, :].reshape(nk, LANES)
        vwin = v_ref[0, pl.ds(ks, NAT_KROWS), u:u + K_COL_SPAN, :].reshape(nk, LANES).astype(BF16)
        for h in range(2):
            kh = jnp.where(sels[h], kwin, 0.0).astype(BF16)
            probs.append(dict(
                j=j, h=h, vwin=vwin,
                s=lax.dot_general(q, kh, nt, preferred_element_type=F32),
                sm=lax.dot_general(q, kmh[h], nt, preferred_element_type=F32)))
    for pr in probs:
        s = pr["s"] + bias_ref[0, pr["h"], pr["j"]]
        sm = pr["sm"]
        mx = jnp.maximum(jnp.max(s, -1, keepdims=True), jnp.max(sm, -1, keepdims=True))
        p = jnp.exp(s - mx)
        pm = jnp.exp(sm - mx)
        pr.update(p=p.astype(BF16), pm=pm.astype(BF16),
                  inv=1.0 / (jnp.sum(p, -1, keepdims=True) + jnp.sum(pm, -1, keepdims=True)))
    for pr in probs:
        pr["o"] = (jnp.dot(pr["p"], pr["vwin"], preferred_element_type=F32)
                   + jnp.dot(pr["pm"], vmeta, preferred_element_type=F32)) * pr["inv"]
    for j in range(ncb):
        qc = slice(j * Q_COL_BLOCK, (j + 1) * Q_COL_BLOCK)
        out = jnp.where(sels[0], probs[2 * j]["o"], probs[2 * j + 1]["o"])
        o_ref[0, :, qc, :] = out.reshape(NAT_ROWS, Q_COL_BLOCK, LANES)


def _natten(proj_f3, meta_f, bias_tab):
    bsz, t_len, _ = proj_f3.shape
    rows = t_len // GRID_W
    assert rows % NAT_ROWS == 0 and rows >= NAT_KROWS
    nb = rows // NAT_ROWS
    hp = NH_B // 2
    ncb = GRID_W // Q_COL_BLOCK
    qb, kb, vb = OFF_QB // LANES, OFF_KB // LANES, OFF_VB // LANES
    proj4 = proj_f3.reshape(bsz, rows, GRID_W, P_F32)

    def pat(i):
        return jnp.where(i == 0, 0, jnp.where(i == nb - 1, 2, 1))

    out = pl.pallas_call(
        _natten_kernel,
        grid=(bsz, hp, nb),
        in_specs=[
            pl.BlockSpec((1, NAT_ROWS, GRID_W, LANES), lambda b, p, i: (b, i, 0, qb + p)),
            pl.BlockSpec((1, rows, GRID_W, LANES), lambda b, p, i: (b, 0, 0, kb + p)),
            pl.BlockSpec((1, rows, GRID_W, LANES), lambda b, p, i: (b, 0, 0, vb + p)),
            pl.BlockSpec((N_META, LANES), lambda b, p, i: (0, kb + p)),
            pl.BlockSpec((N_META, LANES), lambda b, p, i: (0, vb + p)),
            pl.BlockSpec((1, 2, ncb, NAT_ROWS * Q_COL_BLOCK, NAT_KROWS * K_COL_SPAN),
                         lambda b, p, i: (pat(i), p, 0, 0, 0)),
        ],
        out_specs=pl.BlockSpec((1, NAT_ROWS, GRID_W, LANES), lambda b, p, i: (b, i, 0, p)),
        out_shape=jax.ShapeDtypeStruct((bsz, rows, GRID_W, D_B), F32),
        compiler_params=_cparams(("arbitrary", "arbitrary", "arbitrary")),
        name="natten",
    )(proj4, proj4, proj4, meta_f, meta_f, bias_tab)
    return out.reshape(bsz, t_len, D_B)


def _merge_kernel(hf_ref, hb_ref, oa_ref, ga_ref, gb_ref, yb_ref, hg_ref, wa_ref, wb_ref, o_ref, ya_scr):
    sub = hf_ref.shape[0] // MERGE_SUBTILES
    tiles = [slice(s * sub, (s + 1) * sub) for s in range(MERGE_SUBTILES)]
    for rows in tiles:
        for hd in range(NH_A):
            sl = slice(hd * DV_A, (hd + 1) * DV_A)
            h = hf_ref[rows, sl].astype(F32) + hb_ref[rows, sl].astype(F32)
            hc = h - jnp.mean(h, -1, keepdims=True)
            var = jnp.mean(hc * hc, -1, keepdims=True)
            hn = hc * lax.rsqrt(var + LN_EPS) * hg_ref[:, sl]
            ya_scr[rows, sl] = (hn * jax.nn.sigmoid(oa_ref[rows, sl].astype(F32))).astype(BF16)
    d = functools.partial(jnp.dot, preferred_element_type=F32)
    proj = [(d(ya_scr[rows, :], wa_ref[...]), d(yb_ref[rows, :].astype(BF16), wb_ref[...])) for rows in tiles]
    for rows, (a, b) in zip(tiles, proj):
        gate_a = jax.nn.sigmoid(ga_ref[rows, :].astype(F32))
        gate_b = jax.nn.sigmoid(gb_ref[rows, :].astype(F32))
        o_ref[rows, :] = (gate_a * a + gate_b * b).astype(BF16)


def _merge(h_f, h_b, proj_h, y_b, head_g, w_a, w_b):
    n = h_f.shape[0]
    tm = MERGE_TM
    const = lambda i: (0, 0)
    return pl.pallas_call(
        _merge_kernel,
        grid=(n // tm,),
        in_specs=[
            pl.BlockSpec((tm, D_A), lambda i: (i, 0)),
            pl.BlockSpec((tm, D_A), lambda i: (i, 0)),
            pl.BlockSpec((tm, D_A), lambda i: (i, OFF_OA // D_A)),
            pl.BlockSpec((tm, D_MODEL), lambda i: (i, OFF_GA // D_MODEL)),
            pl.BlockSpec((tm, D_MODEL), lambda i: (i, OFF_GB // D_MODEL)),
            pl.BlockSpec((tm, D_B), lambda i: (i, 0)),
            pl.BlockSpec((1, D_A), const),
            pl.BlockSpec((D_A, D_MODEL), const, pipeline_mode=pl.Buffered(1)),
            pl.BlockSpec((D_B, D_MODEL), const, pipeline_mode=pl.Buffered(1)),
        ],
        out_specs=pl.BlockSpec((tm, D_MODEL), lambda i: (i, 0)),
        out_shape=jax.ShapeDtypeStruct((n, D_MODEL), BF16),
        scratch_shapes=[pltpu.VMEM((tm, D_A), BF16)],
        compiler_params=_cparams(("arbitrary",)),
        name="merge",
    )(h_f, h_b, proj_h, proj_h, proj_h, y_b, head_g, w_a, w_b)


def _outproj_router_kernel(m_ref, x_ref, lg_ref, lb_ref, wo_ref, g1_ref, b1_ref, wr_hi_ref, wr_lo_ref, br_ref,
                           h1_ref, route_ref):
    tm = m_ref.shape[0]
    sub = tm // OUT_SUBTILES
    d = functools.partial(jnp.dot, preferred_element_type=F32)
    mixes = [d(m_ref[s * sub:(s + 1) * sub, :], wo_ref[...]) for s in range(OUT_SUBTILES)]
    parts = []
    for s in range(OUT_SUBTILES):
        rows = slice(s * sub, (s + 1) * sub)
        h0 = _layer_norm(x_ref[rows, :], lg_ref[...], lb_ref[...])
        h1 = _layer_norm(ALPHA * h0 + mixes[s], g1_ref[...], b1_ref[...])
        h1_ref[rows, :] = h1
        hi = h1.astype(BF16)
        parts.append((hi, (h1 - hi.astype(F32)).astype(BF16)))
    logits = jnp.concatenate(
        [d(hi, wr_hi_ref[...]) + (d(lo, wr_hi_ref[...]) + d(hi, wr_lo_ref[...])) for hi, lo in parts],
        axis=0) + br_ref[...]

    lane = lax.broadcasted_iota(jnp.int32, logits.shape, 1)
    big = jnp.int32(LANES)
    is_g = lane < N_GROUPS
    gl = jnp.where(is_g, logits, NEG_INF)
    gmax = jnp.max(gl, -1, keepdims=True)
    gsel = jnp.min(jnp.where(is_g & (gl == gmax), lane, big), -1, keepdims=True)
    pg = 1.0 / jnp.sum(jnp.where(is_g, jnp.exp(gl - gmax), 0.0), -1, keepdims=True)
    e_lo = N_GROUPS + gsel * EXPERTS_PER_GROUP
    in_grp = (lane >= e_lo) & (lane < e_lo + EXPERTS_PER_GROUP)
    el = jnp.where(in_grp, logits, NEG_INF)
    v0 = jnp.max(el, -1, keepdims=True)
    i0 = jnp.min(jnp.where(in_grp & (el == v0), lane, big), -1, keepdims=True)
    el1 = jnp.where(lane == i0, NEG_INF, el)
    v1 = jnp.max(el1, -1, keepdims=True)
    i1 = jnp.min(jnp.where(in_grp & (lane != i0) & (el1 == v1), lane, big), -1, keepdims=True)
    e1 = jnp.exp(v1 - v0)
    den = 1.0 + e1
    g0 = pg * (1.0 / den)
    g1 = pg * (e1 / den)
    out = jnp.where(lane == 0, (i0 - N_GROUPS).astype(F32),
                    jnp.where(lane == 1, (i1 - N_GROUPS).astype(F32),
                              jnp.where(lane == 2, g0, jnp.where(lane == 3, g1, 0.0))))
    route_ref[...] = out


def _outproj_router(merged, x, ln_in_g, ln_in_b, w_out, ln1_g, ln1_b, wr_hi, wr_lo, br):
    n = x.shape[0]
    tm = OUT_TM
    const = lambda i: (0, 0)
    row = lambda i: (i, 0)
    return pl.pallas_call(
        _outproj_router_kernel,
        grid=(n // tm,),
        in_specs=[
            pl.BlockSpec((tm, D_MODEL), row),
            pl.BlockSpec((tm, D_MODEL), row),
            pl.BlockSpec((1, D_MODEL), const),
            pl.BlockSpec((1, D_MODEL), const),
            pl.BlockSpec((D_MODEL, D_MODEL), const, pipeline_mode=pl.Buffered(1)),
            pl.BlockSpec((1, D_MODEL), const),
            pl.BlockSpec((1, D_MODEL), const),
            pl.BlockSpec((D_MODEL, LANES), const),
            pl.BlockSpec((D_MODEL, LANES), const),
            pl.BlockSpec((1, LANES), const),
        ],
        out_specs=[pl.BlockSpec((tm, D_MODEL), row), pl.BlockSpec((tm, LANES), row)],
        out_shape=[jax.ShapeDtypeStruct((n, D_MODEL), F32), jax.ShapeDtypeStruct((n, LANES), F32)],
        compiler_params=_cparams(("arbitrary",)),
        name="outproj_router",
    )(merged, x, ln_in_g, ln_in_b, w_out, ln1_g, ln1_b, wr_hi, wr_lo, br)


def _moe_ffn_kernel(be_ref, nu_ref, tok_ref, tokn_ref, h1_hbm, wg_ref, wu_ref, wd_ref, y_ref, xbuf, sem):
    i = pl.program_id(0)
    n_used = nu_ref[0]
    bm = xbuf.shape[1]
    slot = lax.rem(i, 2)
    nxt = 1 - slot

    def start_rows(t_ref, dst_slot, r0, r1):
        for r in range(r0, r1):
            pltpu.make_async_copy(h1_hbm.at[pl.ds(t_ref[0, 0, r], 1)], xbuf.at[dst_slot, pl.ds(r, 1)],
                                  sem.at[dst_slot]).start()

    def wait_rows(s):
        pltpu.make_async_copy(h1_hbm.at[pl.ds(0, bm)], xbuf.at[s], sem.at[s]).wait()

    @pl.when(i == 0)
    def _():
        start_rows(tok_ref, 0, 0, bm)

    @pl.when(i < n_used)
    def _():
        start_rows(tokn_ref, nxt, 0, bm)

    @pl.when(i < n_used)
    def _():
        wait_rows(slot)
        xb = xbuf[slot].astype(BF16)
        tn = (((0,), (1,)), ((), ()))
        tt = (((0,), (0,)), ((), ()))
        g_t = lax.dot_general(wg_ref[0], xb, tn, preferred_element_type=F32)
        u_t = lax.dot_general(wu_ref[0], xb, tn, preferred_element_type=F32)
        h_t = (jax.nn.silu(g_t) * u_t).astype(BF16)
        for half in range(2):
            sl = slice(half * D_EXPERT, (half + 1) * D_EXPERT)
            y_ref[:, sl] = lax.dot_general(wd_ref[0, :, sl], h_t, tt, preferred_element_type=F32).T

    @pl.when(i >= n_used)
    def _():
        @pl.when(i == n_used)
        def _():
            wait_rows(slot)

        y_ref[...] = jnp.zeros(y_ref.shape, F32)


def _moe_ffn(block_e, n_used, tok3, h1, w_gate, w_up, w_down):
    nb = tok3.shape[0]
    bm = MOE_BM
    wmap = lambda i, be, nu: (be[i], 0, 0)
    grid_spec = pltpu.PrefetchScalarGridSpec(
        num_scalar_prefetch=2,
        grid=(nb,),
        in_specs=[
            pl.BlockSpec((1, 1, bm), lambda i, be, nu: (i, 0, 0), memory_space=pltpu.SMEM),
            pl.BlockSpec((1, 1, bm), lambda i, be, nu: (jnp.minimum(i + 1, nb - 1), 0, 0),
                         memory_space=pltpu.SMEM),
            pl.BlockSpec(memory_space=pl.ANY),
            pl.BlockSpec((1, D_MODEL, D_EXPERT), wmap),
            pl.BlockSpec((1, D_MODEL, D_EXPERT), wmap),
            pl.BlockSpec((1, D_EXPERT, D_MODEL), wmap),
        ],
        out_specs=pl.BlockSpec((bm, D_MODEL), lambda i, be, nu: (i, 0)),
        scratch_shapes=[pltpu.VMEM((2, bm, D_MODEL), F32), pltpu.SemaphoreType.DMA((2,))],
    )
    return pl.pallas_call(
        _moe_ffn_kernel,
        grid_spec=grid_spec,
        out_shape=jax.ShapeDtypeStruct((nb * bm, D_MODEL), F32),
        compiler_params=_cparams(("arbitrary",)),
        name="moe_ffn",
    )(block_e, n_used, tok3, tok3, h1, w_gate, w_up, w_down)


def _combine_kernel(pos_ref, posn_ref, y_hbm, h1_ref, route_ref, g_ref, b_ref, o_ref, buf, sem):
    i = pl.program_id(0)
    tm = buf.shape[2]
    slot = lax.rem(i, 2)

    def start_rows(p_ref, s):
        for r in range(tm):
            for k in range(2):
                pltpu.make_async_copy(y_hbm.at[pl.ds(p_ref[0, 0, 2 * r + k], 1)],
                                      buf.at[s, k, pl.ds(r, 1)], sem.at[s]).start(priority=k)

    @pl.when(i == 0)
    def _():
        start_rows(pos_ref, 0)

    @pl.when(i + 1 < pl.num_programs(0))
    def _():
        start_rows(posn_ref, 1 - slot)

    for k in range(2):
        pltpu.make_async_copy(y_hbm.at[pl.ds(0, tm)], buf.at[slot, k], sem.at[slot]).wait()
    route = route_ref[...]
    ffn = buf[slot, 0] * route[:, 2:3] + buf[slot, 1] * route[:, 3:4]
    o_ref[...] = _layer_norm(ALPHA * h1_ref[...] + ffn, g_ref[...], b_ref[...])


def _combine(pos3, y_sorted, h1, route, ln2_g, ln2_b):
    n = h1.shape[0]
    tm = COMB_TM
    nt = n // tm
    const = lambda i: (0, 0)
    row = lambda i: (i, 0)
    return pl.pallas_call(
        _combine_kernel,
        grid=(nt,),
        in_specs=[
            pl.BlockSpec((1, 1, 2 * tm), lambda i: (i, 0, 0), memory_space=pltpu.SMEM),
            pl.BlockSpec((1, 1, 2 * tm), lambda i: (jnp.minimum(i + 1, nt - 1), 0, 0), memory_space=pltpu.SMEM),
            pl.BlockSpec(memory_space=pl.ANY),
            pl.BlockSpec((tm, D_MODEL), row),
            pl.BlockSpec((tm, LANES), row),
            pl.BlockSpec((1, D_MODEL), const),
            pl.BlockSpec((1, D_MODEL), const),
        ],
        out_specs=pl.BlockSpec((tm, D_MODEL), row),
        out_shape=jax.ShapeDtypeStruct((n, D_MODEL), F32),
        scratch_shapes=[pltpu.VMEM((2, 2, tm, D_MODEL), F32), pltpu.SemaphoreType.DMA((2,))],
        compiler_params=_cparams(("arbitrary",)),
        name="combine",
    )(pos3, pos3, y_sorted, h1, route, ln2_g, ln2_b)


def _rank_kernel(route_ref, pstart_ref, pos_ref, carry):
    @pl.when(pl.program_id(0) == 0)
    def _():
        carry[...] = jnp.zeros(carry.shape, F32)

    tm = route_ref.shape[0]
    route = route_ref[...]
    lane = lax.broadcasted_iota(jnp.int32, (tm, LANES), 1)
    oh0 = lane == route[:, 0:1].astype(jnp.int32)
    oh1 = lane == route[:, 1:2].astype(jnp.int32)
    cnt = jnp.where(oh0, 1.0, 0.0) + jnp.where(oh1, 1.0, 0.0)
    r_i = lax.broadcasted_iota(jnp.int32, (tm, tm), 0)
    c_i = lax.broadcasted_iota(jnp.int32, (tm, tm), 1)
    earlier = jnp.dot((c_i < r_i).astype(BF16), cnt.astype(BF16), preferred_element_type=F32)
    row = earlier + (carry[...] + pstart_ref[...])
    pos0 = jnp.sum(jnp.where(oh0, row, 0.0), -1, keepdims=True)
    pos1 = jnp.sum(jnp.where(oh1, row, 0.0), -1, keepdims=True)
    pos_ref[...] = jnp.where(lane == 0, pos0, jnp.where(lane == 1, pos1, 0.0)).astype(jnp.int32)
    carry[...] += jnp.sum(cnt, axis=0, keepdims=True)


def _rank(route, pstart_row):
    n = route.shape[0]
    tm = RANK_TM
    return pl.pallas_call(
        _rank_kernel,
        grid=(n // tm,),
        in_specs=[pl.BlockSpec((tm, LANES), lambda i: (i, 0)), pl.BlockSpec((1, LANES), lambda i: (0, 0))],
        out_specs=pl.BlockSpec((tm, LANES), lambda i: (i, 0)),
        out_shape=jax.ShapeDtypeStruct((n, LANES), jnp.int32),
        scratch_shapes=[pltpu.VMEM((1, LANES), F32)],
        compiler_params=_cparams(("arbitrary",)),
        name="rank",
    )(route, pstart_row)


def _dispatch_plan(route, n):
    bm = MOE_BM
    a = 2 * n
    nb = a // bm + N_EXPERTS
    id_bits = max(a - 1, 1).bit_length()
    flat_e = route[:, 0:2].astype(jnp.int32).reshape(a)
    ids = jnp.arange(a, dtype=jnp.int32)
    order = jnp.sort(flat_e * (1 << id_bits) + ids) & ((1 << id_bits) - 1)
    experts = jnp.arange(N_EXPERTS, dtype=jnp.int32)
    counts = jnp.sum((flat_e[:, None] == experts[None, :]).astype(jnp.int32), axis=0)
    pcounts = (counts + bm - 1) // bm * bm
    incl = (experts[:, None] <= experts[None, :]).astype(jnp.int32)
    pend = pcounts @ incl
    pstart = pend - pcounts
    cstart = counts @ incl - counts
    blk0 = jnp.arange(nb, dtype=jnp.int32) * bm
    block_e = jnp.minimum(jnp.sum((pend[None, :] <= blk0[:, None]).astype(jnp.int32), axis=1),
                          N_EXPERTS - 1)
    off = (blk0 - pstart[block_e])[:, None] + jnp.arange(bm, dtype=jnp.int32)[None, :]
    valid = off < counts[block_e][:, None]
    src = jnp.clip(cstart[block_e][:, None] + off, 0, a - 1)
    tok = jnp.where(valid, order[src] >> 1, 0)
    n_used = (pend[-1:] // bm).astype(jnp.int32)
    pstart_row = jnp.pad(pstart.astype(F32), (0, LANES - N_EXPERTS)).reshape(1, LANES)
    pos = _rank(route, pstart_row)[:, :2]
    return (block_e.astype(jnp.int32), n_used, tok.reshape(nb, 1, bm),
            pos.reshape(n // COMB_TM, 1, 2 * COMB_TM))


def _encode(x, p):
    bsz, t_len, _ = x.shape
    n = bsz * t_len
    x2 = x.reshape(n, D_MODEL)
    proj_f, proj_h, gates = _inproj(x2, p["ln_in_g"], p["ln_in_b"], p["w_main"], p["w_gates"])
    proj_f3 = proj_f.reshape(bsz, t_len, P_F32)
    proj_h3 = proj_h.reshape(bsz, t_len, P_BF16)
    gates3 = gates.reshape(bsz, t_len, LANES)
    gates_row = gates3[:, :, :4 * NH_A].transpose(0, 2, 1)
    qk, k_meta = _conv(proj_f3, p["meta_f"], p["conv_w"], p["conv_b"])
    h_f, h_b = _mlstm(qk, proj_h3, gates3, gates_row, k_meta, p["meta_h"], p["gates_meta"],
                      p["bias_col"], p["bias_row"])
    y_b = _natten(proj_f3, p["meta_f"], p["bias_tab"])
    merged = _merge(h_f.reshape(n, D_A), h_b.reshape(n, D_A), proj_h, y_b.reshape(n, D_B),
                    p["head_g"], p["w_a"], p["w_b"])
    h1, route = _outproj_router(merged, x2, p["ln_in_g"], p["ln_in_b"], p["w_out"], p["ln1_g"], p["ln1_b"],
                                p["wr_hi"], p["wr_lo"], p["br"])
    block_e, n_used, tok3, pos3 = _dispatch_plan(route, n)
    y_sorted = _moe_ffn(block_e, n_used, tok3, h1, p["w_gate"], p["w_up"], p["w_down"])
    out = _combine(pos3, y_sorted, h1, route, p["ln2_g"], p["ln2_b"])
    return out.reshape(bsz, t_len, D_MODEL)


def kernel(x_prompt, x_sample, meta_tokens, ln_in_g, ln_in_b, w_in, b_gates, conv_w, conv_b, head_g, rpb, w_a, w_b, w_out, ln1_g, ln1_b, router_g_w, router_g_b, router_e_w, router_e_b, w_gate, w_up, w_down, ln2_g, ln2_b):
    l = 0
    offs = np.cumsum((0,) + IN_SPLITS)
    col = lambda i: w_in[l][:, offs[i]:offs[i + 1]]
    w_main = jnp.concatenate([col(0), col(1), col(5), col(6), col(7), col(2), col(3), col(8), col(9)],
                             axis=1).astype(BF16)
    w_gates = jnp.pad(col(4), ((0, 0), (0, LANES - 4 * NH_A))).astype(BF16)
    row = lambda v: v.reshape(1, -1).astype(F32)
    wr = jnp.pad(jnp.concatenate([router_g_w[l], router_e_w[l]], axis=1),
                 ((0, 0), (0, LANES - N_GROUPS - N_EXPERTS)))
    wr_hi = wr.astype(BF16)
    p = {
        "ln_in_g": row(ln_in_g), "ln_in_b": row(ln_in_b),
        "w_main": w_main, "w_gates": w_gates,
        "conv_w": conv_w[l], "conv_b": row(conv_b[l]),
        "bias_col": jnp.pad(b_gates[l], (0, LANES - 4 * NH_A)).reshape(1, LANES),
        "bias_row": b_gates[l].reshape(4 * NH_A, 1),
        "head_g": row(head_g[l]),
        "bias_tab": _natten_bias_table(rpb[l]),
        "w_a": w_a[l].astype(BF16), "w_b": w_b[l].astype(BF16), "w_out": w_out[l].astype(BF16),
        "ln1_g": row(ln1_g[l]), "ln1_b": row(ln1_b[l]),
        "wr_hi": wr_hi, "wr_lo": (wr - wr_hi.astype(F32)).astype(BF16),
        "br": jnp.pad(jnp.concatenate([router_g_b[l], router_e_b[l]]),
                      (0, LANES - N_GROUPS - N_EXPERTS)).reshape(1, LANES),
        "w_gate": w_gate[l].astype(BF16), "w_up": w_up[l].astype(BF16), "w_down": w_down[l].astype(BF16),
        "ln2_g": row(ln2_g[l]), "ln2_b": row(ln2_b[l]),
    }
    p["meta_f"], p["meta_h"], p["gates_meta"] = _inproj(meta_tokens, p["ln_in_g"], p["ln_in_b"], w_main, w_gates)
    return (_encode(x_prompt, p), _encode(x_sample, p))
```
